```python
import jax
import jax.numpy as jnp
from jax import lax
import numpy as np

D_MODEL = 2048
BATCH = 1
SEQ = 8192
DEPTH = 2

GRID_W = 64
CTX_LEN = 256
NORM_EPS = 1e-6
N_MOD = 6
POS_BASE = 10000.0

BR = 512
N_BRANCH = 4
FN_GROUPS = 4
FN_GC = BR // FN_GROUPS
RW_HEADS = 8
RW_HD = BR // RW_HEADS
RW_LORA_W = 32
RW_LORA_A = 32
RW_LORA_G = 64
RW_GN_EPS = 64e-5
HG_HEADS = 4
HG_DK = BR // HG_HEADS
HG_DV = BR // HG_HEADS
HG_CHUNK = 16
LRU_BLOCKS = 8
LRU_BD = BR // LRU_BLOCKS
CONV_W = 4
CONV_LEFT = 2
LRU_C = 8.0

COL_A = 0
COL_B = 1
COL_C = 5
COL_D = 10
N_COL_BLOCKS = 12
N_MIX_COLS = N_COL_BLOCKS * BR
N_IN = N_MIX_COLS + N_BRANCH * D_MODEL

N_EXPERTS = 64
N_GROUPS = 8
EXPERTS_PER_GROUP = N_EXPERTS // N_GROUPS
TOP_K = 2
D_EXPERT = 768
MOE_BLOCK = 128

kernel_name = "hybrid_fourier_rwkv7_hgrn2_rglru_moe_dit"


def rms_norm(x, g, eps=NORM_EPS):
    xf = x.astype(jnp.float32)
    y = xf * lax.rsqrt(jnp.mean(xf * xf, axis=-1, keepdims=True) + eps)
    return (y * g.astype(jnp.float32)).astype(x.dtype)


def split_heads(t, n_heads):
    return t.reshape(t.shape[:-1] + (n_heads, t.shape[-1] // n_heads))


def merge_heads(t):
    return t.reshape(t.shape[:-2] + (t.shape[-2] * t.shape[-1],))


def time_shift(u, off):
    L = u.shape[1]
    p = abs(off)
    widths = [(0, 0)] * u.ndim
    widths[1] = (p, p)
    return lax.slice_in_dim(jnp.pad(u, widths), p + off, p + off + L, axis=1)


def bidir_token_shift(u):
    return 0.5 * (time_shift(u, -1) + time_shift(u, 1)) - u


def depthwise_conv(u, w, b):
    out = b
    for j in range(CONV_W):
        out = out + time_shift(u, j - CONV_LEFT) * w[j]
    return out


def prefix_scan(scan_fn, ctx_in, lat_in, s0, reverse):
    if reverse:
        ctx_in = tuple(jnp.flip(t, axis=1) for t in ctx_in)
        lat_in = tuple(jnp.flip(t, axis=1) for t in lat_in)
    y_ctx, s_ctx = scan_fn(*ctx_in, s0)
    y_lat, _ = scan_fn(*lat_in, s_ctx)
    if reverse:
        y_ctx, y_lat = jnp.flip(y_ctx, axis=1), jnp.flip(y_lat, axis=1)
    return y_ctx, y_lat


def grid_sincos(rows):
    t = jnp.arange(rows * GRID_W)
    r = (t // GRID_W).astype(jnp.float32)
    col = (t % GRID_W).astype(jnp.float32)
    nf = D_MODEL // 4
    omega = POS_BASE ** (-jnp.arange(nf, dtype=jnp.float32) / nf)
    ang_r = r[:, None] * omega
    ang_c = col[:, None] * omega
    return jnp.concatenate([jnp.sin(ang_r), jnp.cos(ang_r), jnp.sin(ang_c), jnp.cos(ang_c)], axis=-1)


def fourier_group(u):
    B_, L, _ = u.shape
    ug = u.astype(jnp.float32).reshape(B_, L, FN_GROUPS, FN_GC)
    return jnp.fft.fft2(ug, axes=(1, 3), norm="ortho").real.reshape(B_, L, BR)


def rwkv7_scan(r, wl, k, v, kk, a, s0):
    def step(S, inp):
        r_t, wl_t, k_t, v_t, kk_t, a_t = inp
        S = (S * jnp.exp(wl_t)[:, :, None, :]
             - jnp.einsum("bhvk,bhk->bhv", S, kk_t)[..., None] * (a_t * kk_t)[:, :, None, :]
             + v_t[..., :, None] * k_t[:, :, None, :])
        return S, jnp.einsum("bhvk,bhk->bhv", S, r_t)
    xs = tuple(jnp.moveaxis(t, 1, 0) for t in (r, wl, k, v, kk, a))
    s_fin, ys = lax.scan(step, s0, xs)
    return jnp.moveaxis(ys, 0, 1), s_fin


def rwkv7_mixer(uc, ul, mu, w0, w1, w2, a0, a1, a2, g1, g2, k_k, k_a, r_k, ln_g, ln_b):
    def prep(u):
        u = u.astype(jnp.float32)
        xx = bidir_token_shift(u)
        r = u[:, :, 0] + xx[:, :, 0] * mu[0]
        k = u[:, :, 1] + xx[:, :, 1] * mu[1]
        v = u[:, :, 2] + xx[:, :, 2] * mu[2]
        xw = u[:, :, 3] + xx[:, :, 3] * mu[3]
        xa = u[:, :, 3] + xx[:, :, 3] * mu[4]
        xg = u[:, :, 3] + xx[:, :, 3] * mu[5]
        kk = split_heads(k * k_k, RW_HEADS)
        kk = kk / jnp.maximum(jnp.sqrt(jnp.sum(kk * kk, axis=-1, keepdims=True)), 1e-12)
        p = {"r": split_heads(r, RW_HEADS), "v": split_heads(v, RW_HEADS), "kk": kk,
             "g": jax.nn.sigmoid(xg @ g1) @ g2, "wl": [], "kt": [], "a": []}
        for d in range(2):
            w_raw = w0[d] + jnp.tanh(xw @ w1[d]) @ w2[d]
            a = jax.nn.sigmoid(a0[d] + (xa @ a1[d]) @ a2[d])
            p["wl"].append(split_heads(-jnp.exp(-jax.nn.softplus(-w_raw) - 0.5), RW_HEADS))
            p["kt"].append(split_heads(k * (1.0 + (a - 1.0) * k_a), RW_HEADS))
            p["a"].append(split_heads(a, RW_HEADS))
        return p

    def scan_inputs(p, d):
        return (p["r"], p["wl"][d], p["kt"][d], p["v"], p["kk"], p["a"][d])

    def post(y, p):
        mean = jnp.mean(y, axis=-1, keepdims=True)
        var = jnp.mean(jnp.square(y - mean), axis=-1, keepdims=True)
        y = (y - mean) * lax.rsqrt(var + RW_GN_EPS) * split_heads(ln_g, RW_HEADS) + split_heads(ln_b, RW_HEADS)
        bonus = jnp.sum(p["r"] * (p["kt"][0] + p["kt"][1]) * r_k, axis=-1, keepdims=True) * p["v"]
        return merge_heads(y + bonus) * p["g"]

    pc, pl = prep(uc), prep(ul)
    s0 = jnp.zeros((uc.shape[0], RW_HEADS, RW_HD, RW_HD), jnp.float32)
    y_c, y_l = 0.0, 0.0
    for d in range(2):
        yc_d, yl_d = prefix_scan(rwkv7_scan, scan_inputs(pc, d), scan_inputs(pl, d), s0, reverse=(d == 1))
        y_c, y_l = y_c + yc_d, y_l + yl_d
    return jnp.concatenate([post(y_c, pc), post(y_l, pl)], axis=1)


def hgrn2_chunk_scan(q, lf, k, v, s0):
    B_, L, H, _ = q.shape
    DV = v.shape[-1]
    n = L // HG_CHUNK

    def blocks(t):
        return t.reshape(B_, n, HG_CHUNK, H, t.shape[-1]).transpose(0, 3, 1, 2, 4)

    q, lf, k, v = blocks(q), blocks(lf), blocks(k), blocks(v)
    G = jnp.cumsum(lf, axis=3)
    lower = jnp.tril(jnp.ones((HG_CHUNK, HG_CHUNK), bool))[:, :, None]
    diff = G[:, :, :, :, None, :] - G[:, :, :, None, :, :]
    decay = jnp.exp(jnp.where(lower, diff, -jnp.inf))
    scores = jnp.einsum("bhntc,bhnsc,bhntsc->bhnts", q, k, decay)
    o_intra = jnp.einsum("bhnts,bhnsv->bhntv", scores, v)
    G_last = G[:, :, :, -1:, :]
    U = jnp.einsum("bhncd,bhncv->bhndv", k * jnp.exp(G_last - G), v)
    chunk_decay = jnp.exp(G_last[:, :, :, 0])

    def step(S, inp):
        dec, u = inp
        return dec[..., None] * S + u, S

    s_fin, S_start = lax.scan(step, s0, (jnp.moveaxis(chunk_decay, 2, 0), jnp.moveaxis(U, 2, 0)))
    S_start = jnp.moveaxis(S_start, 0, 2)
    o_inter = jnp.einsum("bhncd,bhndv->bhncv", q * jnp.exp(G), S_start)
    o = (o_intra + o_inter).transpose(0, 2, 3, 1, 4).reshape(B_, L, H, DV)
    return o, s_fin


def hgrn2_mixer(uc, ul, lb, norm_g):
    lb = lb.astype(jnp.float32)
    log_lb, log_1mlb = jnp.log(lb), jnp.log1p(-lb)

    def prep(u):
        u = u.astype(jnp.float32)
        p = {"q": split_heads(jax.nn.silu(u[:, :, 0]), HG_HEADS), "i": split_heads(u[:, :, 3], HG_HEADS),
             "og": u[:, :, 4], "lf": [], "k": []}
        for d in range(2):
            z = u[:, :, 1 + d]
            p["lf"].append(split_heads(jnp.logaddexp(log_lb, log_1mlb + jax.nn.log_sigmoid(z)), HG_HEADS))
            p["k"].append(split_heads((1.0 - lb) * jax.nn.sigmoid(-z), HG_HEADS))
        return p

    def post(y, p):
        y = y * lax.rsqrt(jnp.mean(y * y, axis=-1, keepdims=True) + NORM_EPS) * norm_g
        return merge_heads(y) * jax.nn.silu(p["og"])

    pc, pl = prep(uc), prep(ul)
    s0 = jnp.zeros((uc.shape[0], HG_HEADS, HG_DK, HG_DV), jnp.float32)
    y_c, y_l = 0.0, 0.0
    for d in range(2):
        ins_c = (pc["q"], pc["lf"][d], pc["k"][d], pc["i"])
        ins_l = (pl["q"], pl["lf"][d], pl["k"][d], pl["i"])
        yc_d, yl_d = prefix_scan(hgrn2_chunk_scan, ins_c, ins_l, s0, reverse=(d == 1))
        y_c, y_l = y_c + yc_d, y_l + yl_d
    return jnp.concatenate([post(y_c, pc), post(y_l, pl)], axis=1)


def rglru_scan(a, b, h0):
    def combine(left, right):
        a_l, b_l = left
        a_r, b_r = right
        return a_l * a_r, a_r * b_l + b_r
    a_cum, b_cum = lax.associative_scan(combine, (a, b), axis=1)
    h = a_cum * h0[:, None] + b_cum
    return h, h[:, -1]


def rglru_mixer(uc, ul, conv_w, conv_b, wa, ba, wx, bx, lam):
    def prep(u):
        u = u.astype(jnp.float32)
        xc = depthwise_conv(u[:, :, 0], conv_w, conv_b)
        xb = split_heads(xc, LRU_BLOCKS)
        ins = []
        for d in range(2):
            r = jax.nn.sigmoid(merge_heads(jnp.einsum("blgi,gij->blgj", xb, wa[d])) + ba[d])
            i = jax.nn.sigmoid(merge_heads(jnp.einsum("blgi,gij->blgj", xb, wx[d])) + bx[d])
            log_a = -LRU_C * r * jax.nn.softplus(-lam[d])
            mult = jnp.sqrt(-jnp.expm1(2.0 * log_a))
            ins.append((jnp.exp(log_a), mult * i * xc))
        return ins, jax.nn.gelu(u[:, :, 1])

    (ins_c, gate_c), (ins_l, gate_l) = prep(uc), prep(ul)
    s0 = jnp.zeros((uc.shape[0], BR), jnp.float32)
    y_c, y_l = 0.0, 0.0
    for d in range(2):
        yc_d, yl_d = prefix_scan(rglru_scan, ins_c[d], ins_l[d], s0, reverse=(d == 1))
        y_c, y_l = y_c + yc_d, y_l + yl_d
    return jnp.concatenate([y_c * gate_c, y_l * gate_l], axis=1)


def token_mixers(h_ctx, h_lat, w_in, w_branch, w_out,
                 rw_mu, rw_w0, rw_w1, rw_w2, rw_a0, rw_a1, rw_a2, rw_g1, rw_g2, rw_kk, rw_ka, rw_rk,
                 rw_ln_g, rw_ln_b, hg_lb, hg_norm_g,
                 lru_conv_w, lru_conv_b, lru_wa, lru_ba, lru_wx, lru_bx, lru_lam):
    B_, L, D = h_ctx.shape
    S = L + h_lat.shape[1]
    h_all = jnp.concatenate([h_ctx, h_lat], axis=1)
    proj = h_all @ w_in
    cols = proj[..., :N_MIX_COLS].reshape(B_, S, N_COL_BLOCKS, BR)
    gates = jax.nn.sigmoid(proj[..., N_MIX_COLS:].astype(jnp.float32)).reshape(B_, S, N_BRANCH, D)
    cc, cl = cols[:, :L], cols[:, L:]
    y_a = jnp.concatenate([fourier_group(cc[:, :, COL_A]), fourier_group(cl[:, :, COL_A])], axis=1)
    y_b = rwkv7_mixer(cc[:, :, COL_B:COL_B + 4], cl[:, :, COL_B:COL_B + 4], rw_mu, rw_w0, rw_w1, rw_w2,
                      rw_a0, rw_a1, rw_a2, rw_g1, rw_g2, rw_kk, rw_ka, rw_rk, rw_ln_g, rw_ln_b)
    y_c = hgrn2_mixer(cc[:, :, COL_C:COL_C + 5], cl[:, :, COL_C:COL_C + 5], hg_lb, hg_norm_g)
    y_d = rglru_mixer(cc[:, :, COL_D:COL_D + 2], cl[:, :, COL_D:COL_D + 2], lru_conv_w, lru_conv_b,
                      lru_wa, lru_ba, lru_wx, lru_bx, lru_lam)
    branches = jnp.stack([y_a, y_b, y_c, y_d], axis=2).astype(h_all.dtype)
    proj_br = jnp.einsum("bskc,kcd->bskd", branches, w_branch)
    merged = jnp.sum(gates.astype(h_all.dtype) * proj_br, axis=2)
    out = merged @ w_out
    return out[:, :L], out[:, L:]


def moe_ffn(h, router_w, router_b, w_gate, w_up, w_down):
    N, D = h.shape
    scores = jax.nn.sigmoid(jnp.dot(h.astype(jnp.float32), router_w.astype(jnp.float32)))
    sel = (scores + router_b.astype(jnp.float32)).reshape(N, N_GROUPS, EXPERTS_PER_GROUP)
    group_score = jnp.sum(lax.top_k(sel, TOP_K)[0], axis=-1)
    g_idx = jnp.argmax(group_score, axis=-1)
    in_grp = jnp.take_along_axis(sel, g_idx[:, None, None], axis=1)[:, 0]
    _, e_local = lax.top_k(in_grp, TOP_K)
    e_idx = g_idx[:, None] * EXPERTS_PER_GROUP + e_local
    wts = jnp.take_along_axis(scores, e_idx, axis=1)
    wts = wts / jnp.sum(wts, axis=-1, keepdims=True)
    n_assign = N * TOP_K
    flat_e = e_idx.reshape(-1)
    order = jnp.argsort(flat_e)
    sorted_e = flat_e[order]
    tok = order // TOP_K
    sizes = jnp.bincount(flat_e, length=N_EXPERTS)
    starts = jnp.cumsum(sizes) - sizes
    padded = ((sizes + MOE_BLOCK - 1) // MOE_BLOCK) * MOE_BLOCK
    pad_ends = jnp.cumsum(padded)
    pad_starts = pad_ends - padded
    dest = pad_starts[sorted_e] + jnp.arange(n_assign) - starts[sorted_e]
    n_blocks = -(-n_assign // MOE_BLOCK) + N_EXPERTS
    buf = jnp.zeros((n_blocks * MOE_BLOCK, D), h.dtype).at[dest].set(h[tok])
    blk_e = jnp.minimum(jnp.searchsorted(pad_ends, jnp.arange(n_blocks) * MOE_BLOCK, side="right"),
                        N_EXPERTS - 1)

    def expert_block(args):
        xb, e = args
        return (jax.nn.silu(xb @ w_gate[e]) * (xb @ w_up[e])) @ w_down[e]

    yb = lax.map(expert_block, (buf.reshape(n_blocks, MOE_BLOCK, D), blk_e))
    y = yb.reshape(-1, D)[dest] * wts.reshape(-1)[order][:, None].astype(h.dtype)
    return jnp.zeros_like(h).at[tok].add(y)


def setup_inputs(seed: int = 0) -> dict:
    key = jax.random.key(seed)
    keys = iter(jax.random.split(key, 48))

    def nrm(shape, scale):
        return jax.random.normal(next(keys), shape, jnp.float32) * scale

    def unif(shape, lo, hi):
        return jax.random.uniform(next(keys), shape, jnp.float32, lo, hi)

    D = D_MODEL
    lru_a = unif((DEPTH, 2, BR), 0.9, 0.999) ** (1.0 / LRU_C)
    return {
        "x": nrm((BATCH, SEQ, D), 1.0),
        "c": nrm((BATCH, D), 1.0),
        "ctx": nrm((BATCH, CTX_LEN, D), 1.0),
        "c_ctx": nrm((D,), 1.0),
        "w_mod": nrm((DEPTH, D, N_MOD * D), 0.5 * D ** -0.5),
        "b_mod": nrm((DEPTH, N_MOD * D), 0.02),
        "norm_mix_g": 1.0 + nrm((DEPTH, D), 0.02),
        "norm_ffn_g": 1.0 + nrm((DEPTH, D), 0.02),
        "w_in": nrm((DEPTH, D, N_IN), D ** -0.5),
        "w_branch": nrm((DEPTH, N_BRANCH, BR, D), BR ** -0.5),
        "w_out": nrm((DEPTH, D, D), D ** -0.5),
        "rw_mu": unif((DEPTH, 6, BR), 0.0, 1.0),
        "rw_w0": unif((DEPTH, 2, BR), -6.0, -1.0),
        "rw_w1": nrm((DEPTH, 2, BR, RW_LORA_W), BR ** -0.5),
        "rw_w2": nrm((DEPTH, 2, RW_LORA_W, BR), 0.1 * RW_LORA_W ** -0.5),
        "rw_a0": nrm((DEPTH, 2, BR), 0.1),
        "rw_a1": nrm((DEPTH, 2, BR, RW_LORA_A), BR ** -0.5),
        "rw_a2": nrm((DEPTH, 2, RW_LORA_A, BR), RW_LORA_A ** -0.5),
        "rw_g1": nrm((DEPTH, BR, RW_LORA_G), BR ** -0.5),
        "rw_g2": nrm((DEPTH, RW_LORA_G, BR), RW_LORA_G ** -0.5),
        "rw_kk": 0.85 + nrm((DEPTH, BR), 0.05),
        "rw_ka": 1.0 + nrm((DEPTH, BR), 0.05),
        "rw_rk": nrm((DEPTH, RW_HEADS, RW_HD), 0.1),
        "rw_ln_g": 1.0 + nrm((DEPTH, BR), 0.02),
        "rw_ln_b": nrm((DEPTH, BR), 0.02),
        "hg_lb_logits": nrm((DEPTH, BR), 0.1),
        "hg_norm_g": 1.0 + nrm((DEPTH, HG_DV), 0.02),
        "lru_conv_w": nrm((DEPTH, CONV_W, BR), CONV_W ** -0.5),
        "lru_conv_b": nrm((DEPTH, BR), 0.02),
        "lru_wa": nrm((DEPTH, 2, LRU_BLOCKS, LRU_BD, LRU_BD), LRU_BD ** -0.5),
        "lru_ba": nrm((DEPTH, 2, BR), 0.02),
        "lru_wx": nrm((DEPTH, 2, LRU_BLOCKS, LRU_BD, LRU_BD), LRU_BD ** -0.5),
        "lru_bx": nrm((DEPTH, 2, BR), 0.02),
        "lru_lam": jnp.log(lru_a) - jnp.log1p(-lru_a),
        "router_w": nrm((D, N_EXPERTS), D ** -0.5),
        "router_b": nrm((N_EXPERTS,), 0.01),
        "moe_w_gate": nrm((DEPTH, N_EXPERTS, D, D_EXPERT), D ** -0.5),
        "moe_w_up": nrm((DEPTH, N_EXPERTS, D, D_EXPERT), D ** -0.5),
        "moe_w_down": nrm((DEPTH, N_EXPERTS, D_EXPERT, D), D_EXPERT ** -0.5),
        "final_norm_g": 1.0 + nrm((D,), 0.02),
    }


def reference(x, c, ctx, c_ctx, w_mod, b_mod, norm_mix_g, norm_ffn_g, w_in, w_branch, w_out,
              rw_mu, rw_w0, rw_w1, rw_w2, rw_a0, rw_a1, rw_a2, rw_g1, rw_g2, rw_kk, rw_ka, rw_rk,
              rw_ln_g, rw_ln_b, hg_lb_logits, hg_norm_g, lru_conv_w, lru_conv_b, lru_wa, lru_ba,
              lru_wx, lru_bx, lru_lam, router_w, router_b, moe_w_gate, moe_w_up, moe_w_down,
              final_norm_g):
    B_, T, D = x.shape
    L = ctx.shape[1]
    ROWS = T // GRID_W
    x = x + grid_sincos(ROWS).astype(x.dtype)[None]
    lb_cum = jnp.cumsum(jax.nn.softmax(hg_lb_logits.astype(jnp.float32), axis=0), axis=0)
    hg_lb = lb_cum - lb_cum[:1]
    sc, scc = jax.nn.silu(c), jax.nn.silu(c_ctx)
    for l in range(DEPTH):
        last = l == DEPTH - 1
        m_lat = (sc @ w_mod[l] + b_mod[l]).reshape(B_, N_MOD, 1, D)
        m_ctx = (scc @ w_mod[l] + b_mod[l]).reshape(N_MOD, D)
        h_lat = rms_norm(x, norm_mix_g[l]) * (1.0 + m_lat[:, 1]) + m_lat[:, 0]
        h_ctx = rms_norm(ctx, norm_mix_g[l]) * (1.0 + m_ctx[1]) + m_ctx[0]
        mix_ctx, mix_lat = token_mixers(
            h_ctx, h_lat, w_in[l], w_branch[l], w_out[l],
            rw_mu[l], rw_w0[l], rw_w1[l], rw_w2[l], rw_a0[l], rw_a1[l], rw_a2[l], rw_g1[l], rw_g2[l],
            rw_kk[l], rw_ka[l], rw_rk[l], rw_ln_g[l], rw_ln_b[l], hg_lb[l], hg_norm_g[l],
            lru_conv_w[l], lru_conv_b[l], lru_wa[l], lru_ba[l], lru_wx[l], lru_bx[l], lru_lam[l])
        x = x + m_lat[:, 2] * mix_lat
        h_lat = rms_norm(x, norm_ffn_g[l]) * (1.0 + m_lat[:, 4]) + m_lat[:, 3]
        if last:
            y_lat = moe_ffn(h_lat.reshape(-1, D), router_w, router_b,
                            moe_w_gate[l], moe_w_up[l], moe_w_down[l]).reshape(B_, T, D)
            x = x + m_lat[:, 5] * y_lat
        else:
            ctx = ctx + m_ctx[2] * mix_ctx
            h_ctx = rms_norm(ctx, norm_ffn_g[l]) * (1.0 + m_ctx[4]) + m_ctx[3]
            h_all = jnp.concatenate([h_ctx, h_lat], axis=1).reshape(-1, D)
            y = moe_ffn(h_all, router_w, router_b,
                        moe_w_gate[l], moe_w_up[l], moe_w_down[l]).reshape(B_, L + T, D)
            ctx = ctx + m_ctx[5] * y[:, :L]
            x = x + m_lat[:, 5] * y[:, L:]
    return rms_norm(x, final_norm_g)
```

```python
import functools
import math

import numpy as np
import jax
import jax.numpy as jnp
from jax import lax
from jax.experimental import pallas as pl
from jax.experimental.pallas import tpu as pltpu

F32 = jnp.float32
BF16 = jnp.bfloat16
HIGHEST = lax.Precision.HIGHEST

NORM_EPS = 1e-6
GRID_W = 64
POS_BASE = 10000.0
N_MOD = 6
BR = 512
N_BRANCH = 4
FN_GROUPS = 4
FN_GC = BR // FN_GROUPS
RW_HEADS = 8
RW_HD = BR // RW_HEADS
RW_GN_EPS = 64e-5
HG_HEADS = 4
HG_DK = BR // HG_HEADS
LRU_BLOCKS = 8
LRU_C = 8.0
COL_A, COL_B, COL_C, COL_D = 0, 1, 5, 10
N_COL_BLOCKS = 12
N_EXPERTS = 64
N_GROUPS = 8
EXPERTS_PER_GROUP = N_EXPERTS // N_GROUPS
TOP_K = 2
MOE_BLOCK = 128

ROW_BLOCK = 256
CHUNK = 64
FFT_N2 = 64
VMEM_LIMIT = 56 * 1024 * 1024
NEG_BIG = -1e30


def _cparams(n_axes):
    return pltpu.CompilerParams(dimension_semantics=("arbitrary",) * n_axes,
                                vmem_limit_bytes=VMEM_LIMIT)


def _rev_block(g, n_ctx_blocks, n_blocks):
    return jnp.where(g < n_ctx_blocks, n_ctx_blocks - 1 - g, n_blocks - 1 - (g - n_ctx_blocks))


def _row_class(i, n_ctx_blocks):
    return jnp.where(i < n_ctx_blocks, 0, 1)


def _sigmoid(x):
    return jax.nn.sigmoid(x)


def _silu(x):
    return x * jax.nn.sigmoid(x)


def _log_sigmoid(x):
    return jnp.minimum(x, 0.0) - jnp.log1p(jnp.exp(-jnp.abs(x)))


def _softplus(x):
    return jnp.maximum(x, 0.0) + jnp.log1p(jnp.exp(-jnp.abs(x)))


def _dot(a, b):
    return jnp.dot(a.astype(BF16), b.astype(BF16), preferred_element_type=F32)


def _dot_nt(a, b):
    return lax.dot_general(a.astype(BF16), b.astype(BF16), (((1,), (1,)), ((), ())),
                           preferred_element_type=F32)


def _dot_tn(a, b):
    return lax.dot_general(a.astype(BF16), b.astype(BF16), (((0,), (0,)), ((), ())),
                           preferred_element_type=F32)


def _dot_hi(a, b):
    return jnp.dot(a, b, precision=HIGHEST, preferred_element_type=F32)


def _mm_body(*refs, act, has_bias, has_res):
    a_ref, b_ref = refs[0], refs[1]
    k = 2
    bias_ref = res_ref = scale_ref = None
    if has_bias:
        bias_ref = refs[k]
        k += 1
    if has_res:
        res_ref, scale_ref = refs[k], refs[k + 1]
        k += 2
    o_ref, wb_ref = refs[k], refs[k + 1]

    @pl.when(pl.program_id(1) == 0)
    def _cast_weights():
        wb_ref[...] = b_ref[...].astype(BF16)

    acc = jnp.dot(a_ref[...].astype(BF16), wb_ref[...], preferred_element_type=F32)
    if has_bias:
        acc = acc + bias_ref[...]
    if act == "sigmoid":
        acc = _sigmoid(acc)
    if has_res:
        acc = res_ref[...] + scale_ref[...] * acc
    o_ref[...] = acc.astype(o_ref.dtype)


def _pick_tm(m):
    return next(t for t in (768, 512, 256, 128, 64) if m % t == 0)


def _mm(a, b, *, n_out, tm, tn, out_dtype, layer=None, col0=0, act=None, bias=None,
        res=None, scale=None, n_ctx_blocks=1):
    M, K = a.shape
    assert M % tm == 0 and n_out % tn == 0
    grid = (n_out // tn, M // tm)
    in_specs = [pl.BlockSpec((tm, K), lambda j, i: (i, 0))]
    if layer is None:
        in_specs.append(pl.BlockSpec((K, tn), lambda j, i: (0, j + col0)))
    else:
        in_specs.append(pl.BlockSpec((None, K, tn), lambda j, i: (layer, 0, j + col0)))
    args = [a, b]
    if bias is not None:
        in_specs.append(pl.BlockSpec((1, tn), lambda j, i: (0, j)))
        args.append(bias)
    if res is not None:
        in_specs.append(pl.BlockSpec((tm, tn), lambda j, i: (i, j)))
        in_specs.append(pl.BlockSpec((None, 1, tn), lambda j, i: (_row_class(i, n_ctx_blocks), 0, j)))
        args += [res, scale]
    return pl.pallas_call(
        functools.partial(_mm_body, act=act, has_bias=bias is not None, has_res=res is not None),
        grid=grid,
        in_specs=in_specs,
        out_specs=pl.BlockSpec((tm, tn), lambda j, i: (i, j)),
        out_shape=jax.ShapeDtypeStruct((M, n_out), out_dtype),
        scratch_shapes=[pltpu.VMEM((K, tn), BF16)],
        compiler_params=_cparams(2),
    )(*args)


def _norm_body(*refs, shift_idx, scale_idx, with_router):
    if with_router:
        x_ref, g_ref, mod_ref, rw_ref, o_ref, lg_ref = refs
    else:
        x_ref, g_ref, mod_ref, o_ref = refs
    x = x_ref[...]
    y = x * lax.rsqrt(jnp.mean(x * x, axis=-1, keepdims=True) + NORM_EPS) * g_ref[...]
    if scale_idx is not None:
        m = mod_ref[...]
        y = y * (1.0 + m[scale_idx:scale_idx + 1]) + m[shift_idx:shift_idx + 1]
    o_ref[...] = y.astype(o_ref.dtype)
    if with_router:
        lg_ref[...] = lax.dot_general(rw_ref[...], y, (((1,), (1,)), ((), ())), precision=HIGHEST,
                                      preferred_element_type=F32)


def _norm_mod(x, g, mod, *, shift_idx, scale_idx, out_dtype, n_ctx_blocks, router_w=None):
    S, D = x.shape
    tm = ROW_BLOCK
    in_specs = [pl.BlockSpec((tm, D), lambda i: (i, 0)),
                pl.BlockSpec((1, D), lambda i: (0, 0)),
                pl.BlockSpec((None, 8, D), lambda i: (_row_class(i, n_ctx_blocks), 0, 0))]
    args = [x, g, mod]
    out_specs = pl.BlockSpec((tm, D), lambda i: (i, 0))
    out_shape = jax.ShapeDtypeStruct((S, D), out_dtype)
    if router_w is not None:
        in_specs.append(pl.BlockSpec((N_EXPERTS, D), lambda i: (0, 0)))
        args.append(router_w.T)
        out_specs = [out_specs, pl.BlockSpec((N_EXPERTS, tm), lambda i: (0, i))]
        out_shape = [out_shape, jax.ShapeDtypeStruct((N_EXPERTS, S), F32)]
    return pl.pallas_call(
        functools.partial(_norm_body, shift_idx=shift_idx, scale_idx=scale_idx,
                          with_router=router_w is not None),
        grid=(S // tm,),
        in_specs=in_specs,
        out_specs=out_specs,
        out_shape=out_shape,
        compiler_params=_cparams(1),
    )(*args)


def _dft_tables(n, scale):
    k = jnp.arange(n, dtype=jnp.int32)
    ang = ((k[:, None] * k[None, :]) % n).astype(F32) * (2.0 * math.pi / n)
    return jnp.cos(ang) * scale, jnp.sin(ang) * scale


def _channel_dft_tables():
    c, s = _dft_tables(FN_GC, FN_GC ** -0.5)
    eye = jnp.eye(FN_GROUPS, dtype=F32)
    return jnp.kron(eye, c), jnp.kron(eye, s)


def _fnet_dense_body(u_ref, cc_ref, sc_ref, cl_ref, sl_ref, o_ref):
    u = u_ref[...]
    ac = _dot_hi(u, cc_ref[...])
    as_ = _dot_hi(u, sc_ref[...])
    o_ref[...] = _dot_hi(cl_ref[...], ac) - _dot_hi(sl_ref[...], as_)


def _fnet_dense(u, cc, sc):
    L = u.shape[0]
    cl, sl = _dft_tables(L, L ** -0.5)
    return pl.pallas_call(
        _fnet_dense_body,
        out_shape=jax.ShapeDtypeStruct((L, BR), F32),
        compiler_params=pltpu.CompilerParams(vmem_limit_bytes=VMEM_LIMIT),
    )(u, cc, sc, cl, sl)


def _fnet_stage1_body(u_ref, gr_ref, gi_ref, cc_ref, sc_ref, yr_ref, yi_ref):
    u = u_ref[...]
    zr = _dot_hi(u, cc_ref[...])
    zi = -_dot_hi(u, sc_ref[...])
    gr, gi = gr_ref[...], gi_ref[...]
    yr_ref[...] = _dot_hi(gr, zr) - _dot_hi(gi, zi)
    yi_ref[...] = _dot_hi(gr, zi) + _dot_hi(gi, zr)


def _fnet_stage2_body(yr_ref, yi_ref, c2_ref, s2_ref, o_ref):
    c2, s2 = c2_ref[...], s2_ref[...]
    for j in range(8):
        o_ref[:, j, :] = _dot_hi(c2, yr_ref[j]) + _dot_hi(s2, yi_ref[j])


def _fnet_two_stage(u, cc, sc):
    L = u.shape[0]
    n2 = FFT_N2
    n1 = L // n2
    assert n1 * n2 == L and n1 % 8 == 0
    f1 = jnp.arange(n1, dtype=jnp.int32)
    t1 = jnp.arange(n1, dtype=jnp.int32)
    t2 = jnp.arange(n2, dtype=jnp.int32)
    t = n2 * t1[None, None, :] + t2[:, None, None]
    ang = ((f1[None, :, None] * t) % L).astype(F32) * (2.0 * math.pi / L)
    gr = jnp.cos(ang) * (L ** -0.5)
    gi = -jnp.sin(ang) * (L ** -0.5)
    c2, s2 = _dft_tables(n2, 1.0)
    u2 = u.reshape(n1, n2 * BR)
    yr, yi = pl.pallas_call(
        _fnet_stage1_body,
        grid=(n2,),
        in_specs=[pl.BlockSpec((n1, BR), lambda j: (0, j)),
                  pl.BlockSpec((None, n1, n1), lambda j: (j, 0, 0)),
                  pl.BlockSpec((None, n1, n1), lambda j: (j, 0, 0)),
                  pl.BlockSpec((BR, BR), lambda j: (0, 0)),
                  pl.BlockSpec((BR, BR), lambda j: (0, 0))],
        out_specs=[pl.BlockSpec((n1, BR), lambda j: (0, j)),
                   pl.BlockSpec((n1, BR), lambda j: (0, j))],
        out_shape=[jax.ShapeDtypeStruct((n1, n2 * BR), F32)] * 2,
        compiler_params=_cparams(1),
    )(u2, gr, gi, cc, sc)
    out = pl.pallas_call(
        _fnet_stage2_body,
        grid=(n1 // 8,),
        in_specs=[pl.BlockSpec((8, n2, BR), lambda i: (i, 0, 0)),
                  pl.BlockSpec((8, n2, BR), lambda i: (i, 0, 0)),
                  pl.BlockSpec((n2, n2), lambda i: (0, 0)),
                  pl.BlockSpec((n2, n2), lambda i: (0, 0))],
        out_specs=pl.BlockSpec((n2, 8, BR), lambda i: (0, i, 0)),
        out_shape=jax.ShapeDtypeStruct((n2, n1, BR), F32),
        compiler_params=_cparams(1),
    )(yr.reshape(n1, n2, BR), yi.reshape(n1, n2, BR), c2, s2)
    return out.reshape(L, BR)


def _fnet_mixer(cols, n_ctx):
    cc, sc = _channel_dft_tables()
    ua = cols[:, COL_A * BR:(COL_A + 1) * BR]
    y_ctx = _fnet_dense(ua[:n_ctx], cc, sc)
    y_lat = _fnet_two_stage(ua[n_ctx:], cc, sc)
    return jnp.concatenate([y_ctx, y_lat], axis=0).astype(BF16)


def _halo_rows(x, n_ctx, tb, before, after):
    C = x.shape[1]
    parts = []
    for seg in (x[:n_ctx], x[n_ctx:]):
        blk = seg.reshape(-1, tb, C)
        prev = jnp.concatenate([jnp.zeros((1, before, C), x.dtype), blk[:-1, tb - before:]], axis=0)
        nxt = jnp.concatenate([blk[1:, :after], jnp.zeros((1, after, C), x.dtype)], axis=0)
        pad = jnp.zeros((blk.shape[0], 8 - before - after, C), x.dtype)
        parts.append(jnp.concatenate([prev, nxt, pad], axis=1))
    return jnp.concatenate(parts, axis=0)


def _shift_rows(x, halo_row, down):
    n = x.shape[0]
    row = lax.broadcasted_iota(jnp.int32, (n, 1), 0)
    if down:
        return jnp.where(row == 0, halo_row, pltpu.roll(x, 1, axis=0))
    return jnp.where(row == n - 1, halo_row, pltpu.roll(x, n - 1, axis=0))


def _lru_prep_body(x_ref, halo_ref, cw_ref, cb_ref, w4_ref, b4_ref, lam_ref,
                   af_ref, bf_ref, ab_ref, bb_ref):
    x = x_ref[...]
    halo = halo_ref[...]
    x_m1 = _shift_rows(x, halo[1:2], True)
    x_m2 = _shift_rows(x_m1, halo[0:1], True)
    x_p1 = _shift_rows(x, halo[2:3], False)
    cw = cw_ref[...]
    xc = cb_ref[...] + x_m2 * cw[0:1] + x_m1 * cw[1:2] + x * cw[2:3] + x_p1 * cw[3:4]
    pre = _sigmoid(_dot(xc, w4_ref[...]) + b4_ref[...])
    sp = _softplus(-lam_ref[...])
    for d, (a_ref, b_ref) in enumerate(((af_ref, bf_ref), (ab_ref, bb_ref))):
        r = pre[:, d * BR:(d + 1) * BR]
        i = pre[:, (2 + d) * BR:(3 + d) * BR]
        log_a = (-LRU_C) * r * sp[d:d + 1]
        a_ref[...] = jnp.exp(log_a)
        b_ref[...] = jnp.sqrt(jnp.maximum(1.0 - jnp.exp(2.0 * log_a), 0.0)) * i * xc


def _lru_scan_body(af_ref, bf_ref, ab_ref, bb_ref, hf_ref, hb_ref, sf_ref, sb_ref):
    @pl.when(pl.program_id(0) == 0)
    def _init():
        sf_ref[...] = jnp.zeros_like(sf_ref)
        sb_ref[...] = jnp.zeros_like(sb_ref)

    tb = af_ref.shape[0]

    def step(t, carry):
        hf, hb = carry
        hf = af_ref[pl.ds(t, 1), :] * hf + bf_ref[pl.ds(t, 1), :]
        hf_ref[pl.ds(t, 1), :] = hf
        u = tb - 1 - t
        hb = ab_ref[pl.ds(u, 1), :] * hb + bb_ref[pl.ds(u, 1), :]
        hb_ref[pl.ds(u, 1), :] = hb
        return hf, hb

    hf, hb = lax.fori_loop(0, tb, step, (sf_ref[...], sb_ref[...]), unroll=8)
    sf_ref[...] = hf
    sb_ref[...] = hb


def _gelu_tanh(x):
    return 0.5 * x * (1.0 + jnp.tanh(math.sqrt(2.0 / math.pi) * (x + 0.044715 * (x * x * x))))


def _lru_post_body(hf_ref, hb_ref, g_ref, o_ref):
    o_ref[...] = ((hf_ref[...] + hb_ref[...]) * _gelu_tanh(g_ref[...])).astype(o_ref.dtype)


def _block_diag(w):
    G, n, m = w.shape
    eye = jnp.eye(G, dtype=w.dtype)
    return (eye[:, None, :, None] * w[:, :, None, :]).reshape(G * n, G * m)


def _rglru_mixer(cols, n_ctx, conv_w, conv_b, wa, ba, wx, bx, lam):
    S = cols.shape[0]
    tb = ROW_BLOCK
    nblk, ncb = S // tb, n_ctx // tb
    halo = _halo_rows(cols[:, COL_D * BR:(COL_D + 1) * BR], n_ctx, tb, 2, 1)
    w4 = jnp.concatenate([_block_diag(wa[0]), _block_diag(wa[1]),
                          _block_diag(wx[0]), _block_diag(wx[1])], axis=1).astype(BF16)
    b4 = jnp.concatenate([ba[0], ba[1], bx[0], bx[1]])[None]
    cw8 = jnp.zeros((8, BR), F32).at[:4].set(conv_w)
    lam8 = jnp.zeros((8, BR), F32).at[:2].set(lam)
    row = lambda c: pl.BlockSpec((tb, BR), lambda i: (i, c))
    full = lambda shp: pl.BlockSpec(shp, lambda i: (0,) * len(shp))
    a_f, b_f, a_b, b_b = pl.pallas_call(
        _lru_prep_body,
        grid=(nblk,),
        in_specs=[row(COL_D), pl.BlockSpec((None, 8, BR), lambda i: (i, 0, 0)),
                  full((8, BR)), full((1, BR)), full((BR, 4 * BR)), full((1, 4 * BR)), full((8, BR))],
        out_specs=[row(0)] * 4,
        out_shape=[jax.ShapeDtypeStruct((S, BR), F32)] * 4,
        compiler_params=_cparams(1),
    )(cols, halo, cw8, conv_b[None], w4, b4, lam8)
    fwd = pl.BlockSpec((tb, BR), lambda g: (g, 0))
    bwd = pl.BlockSpec((tb, BR), lambda g: (_rev_block(g, ncb, nblk), 0))
    h_f, h_b = pl.pallas_call(
        _lru_scan_body,
        grid=(nblk,),
        in_specs=[fwd, fwd, bwd, bwd],
        out_specs=[fwd, bwd],
        out_shape=[jax.ShapeDtypeStruct((S, BR), F32)] * 2,
        scratch_shapes=[pltpu.VMEM((1, BR), F32), pltpu.VMEM((1, BR), F32)],
        compiler_params=_cparams(1),
    )(a_f, b_f, a_b, b_b)
    return pl.pallas_call(
        _lru_post_body,
        grid=(nblk,),
        in_specs=[row(0), row(0), row(COL_D + 1)],
        out_specs=row(0),
        out_shape=jax.ShapeDtypeStruct((S, BR), BF16),
        compiler_params=_cparams(1),
    )(h_f, h_b, cols)


def _level_sizes(c):
    out, b = [], c // 2
    while b >= 1:
        out.append(b)
        b //= 2
    return out


def _hgrn_tables(c):
    t = np.arange(c)
    sizes = _level_sizes(c)
    mats = np.zeros((2, (1 + len(sizes)) * c, c), np.float32)
    masks = np.zeros((len(sizes), c, c), np.float32)
    cum = [(t[None, :] <= t[:, None]).astype(np.float32), (t[None, :] >= t[:, None]).astype(np.float32)]
    for d in range(2):
        mats[d, :c] = cum[d]
        for l, b in enumerate(sizes):
            ref = (t // (2 * b)) * (2 * b) + (b - 1 if d == 0 else b)
            mats[d, (l + 1) * c:(l + 2) * c] = cum[d] - cum[d][ref]
    for l, b in enumerate(sizes):
        masks[l] = (t[:, None] // (2 * b) == t[None, :] // (2 * b)).astype(np.float32)
    return jnp.asarray(mats), jnp.asarray(masks)


def _hgrn_chunk(q_raw, z, v, lb_ref, mats, masks_ref, st_ref, rev):
    c = q_raw.shape[0]
    sizes = _level_sizes(c)
    log_lb, log_1mlb, one_m_lb = lb_ref[0:1], lb_ref[1:2], lb_ref[2:3]
    q = _silu(q_raw)
    a = log_lb
    b = log_1mlb + _log_sigmoid(z)
    lf = jnp.maximum(a, b) + jnp.log1p(jnp.exp(-jnp.abs(a - b)))
    k = one_m_lb * _sigmoid(-z)
    dall = _dot_hi(mats, lf)
    g = dall[:c]
    g_last = g[0:1] if rev else g[c - 1:c]
    qg = q * jnp.exp(g)
    kg = k * jnp.exp(g_last - g)
    dec = jnp.exp(g_last)
    row = lax.broadcasted_iota(jnp.int32, (c, 1), 0)
    eye = (lax.broadcasted_iota(jnp.int32, (c, c), 0) == lax.broadcasted_iota(jnp.int32, (c, c), 1))
    q_lv, k_lv = [], []
    for l, b in enumerate(sizes):
        dl = dall[(l + 1) * c:(l + 2) * c]
        second = ((row >> int(math.log2(b))) & 1) == 1
        q_side = jnp.logical_not(second) if rev else second
        k_side = jnp.logical_not(q_side)
        q_lv.append(q * jnp.exp(jnp.where(q_side, dl, NEG_BIG)))
        k_lv.append(k * jnp.exp(jnp.where(k_side, -dl, NEG_BIG)))
    qk = q * k
    ys = []
    for h in range(HG_HEADS):
        sl = slice(h * HG_DK, (h + 1) * HG_DK)
        st = st_ref[h]
        scores = jnp.where(eye, jnp.sum(qk[:, sl], axis=-1, keepdims=True), 0.0)
        for l in range(len(sizes)):
            scores = scores + masks_ref[l] * _dot_nt(q_lv[l][:, sl], k_lv[l][:, sl])
        y = _dot_nt(qg[:, sl], st) + _dot(scores, v[:, sl])
        st_ref[h] = st * dec[:, sl] + _dot_tn(v[:, sl], kg[:, sl])
        ys.append(y)
    return jnp.concatenate(ys, axis=-1)


def _hgrn_scan_body(qf_ref, zf_ref, vf_ref, qb_ref, zb_ref, vb_ref, lb_ref, mats_ref, masks_ref,
                    yf_ref, yb_ref, sf_ref, sb_ref):
    @pl.when(pl.program_id(0) == 0)
    def _init():
        sf_ref[...] = jnp.zeros_like(sf_ref)
        sb_ref[...] = jnp.zeros_like(sb_ref)

    yf_ref[...] = _hgrn_chunk(qf_ref[...], zf_ref[...], vf_ref[...], lb_ref, mats_ref[0],
                              masks_ref, sf_ref, False)
    yb_ref[...] = _hgrn_chunk(qb_ref[...], zb_ref[...], vb_ref[...], lb_ref, mats_ref[1],
                              masks_ref, sb_ref, True)


def _hgrn_post_body(yf_ref, yb_ref, og_ref, ng_ref, o_ref):
    y = yf_ref[...] + yb_ref[...]
    outs = []
    for h in range(HG_HEADS):
        yh = y[:, h * HG_DK:(h + 1) * HG_DK]
        outs.append(yh * lax.rsqrt(jnp.mean(yh * yh, axis=-1, keepdims=True) + NORM_EPS) * ng_ref[...])
    o_ref[...] = (jnp.concatenate(outs, axis=-1) * _silu(og_ref[...])).astype(o_ref.dtype)


def _hgrn2_mixer(cols, n_ctx, lb, norm_g):
    S = cols.shape[0]
    c = CHUNK
    nch, ncc = S // c, n_ctx // c
    mats, masks = _hgrn_tables(c)
    lb8 = jnp.zeros((8, BR), F32).at[0].set(jnp.log(lb)).at[1].set(jnp.log1p(-lb)).at[2].set(1.0 - lb)
    fwd = lambda col: pl.BlockSpec((c, BR), lambda g: (g, col))
    bwd = lambda col: pl.BlockSpec((c, BR), lambda g: (_rev_block(g, ncc, nch), col))
    full = lambda shp: pl.BlockSpec(shp, lambda g: (0,) * len(shp))
    y_f, y_b = pl.pallas_call(
        _hgrn_scan_body,
        grid=(nch,),
        in_specs=[fwd(COL_C), fwd(COL_C + 1), fwd(COL_C + 3),
                  bwd(COL_C), bwd(COL_C + 2), bwd(COL_C + 3),
                  full((8, BR)), full(mats.shape), full(masks.shape)],
        out_specs=[fwd(0), bwd(0)],
        out_shape=[jax.ShapeDtypeStruct((S, BR), F32)] * 2,
        scratch_shapes=[pltpu.VMEM((HG_HEADS, HG_DK, HG_DK), F32)] * 2,
        compiler_params=_cparams(1),
    )(cols, cols, cols, cols, cols, cols, lb8, mats, masks)
    tb = ROW_BLOCK
    row = lambda col: pl.BlockSpec((tb, BR), lambda i: (i, col))
    return pl.pallas_call(
        _hgrn_post_body,
        grid=(S // tb,),
        in_specs=[row(0), row(0), row(COL_C + 4), pl.BlockSpec((1, HG_DK), lambda i: (0, 0))],
        out_specs=row(0),
        out_shape=jax.ShapeDtypeStruct((S, BR), BF16),
        compiler_params=_cparams(1),
    )(y_f, y_b, cols, norm_g[None])


def _rw_prep_body(r_ref, k_ref, v_ref, x_ref, halo_ref, mu_ref, vec_ref, w1_ref, w2_ref, a1_ref,
                  a2_ref, g1_ref, g2_ref, bd_ref,
                  ro_ref, vo_ref, kko_ref, go_ref, wl0_ref, wl1_ref, kt0_ref, kt1_ref,
                  a0o_ref, a1o_ref):
    halo = halo_ref[...]
    mu = mu_ref[...]
    vec = vec_ref[...]

    def shifted(ref, j):
        u = ref[...]
        prev = _shift_rows(u, halo[0:1, j * BR:(j + 1) * BR], True)
        nxt = _shift_rows(u, halo[1:2, j * BR:(j + 1) * BR], False)
        return u, 0.5 * (prev + nxt) - u

    ur, xr = shifted(r_ref, 0)
    uk, xk = shifted(k_ref, 1)
    uv, xv = shifted(v_ref, 2)
    ux, xx = shifted(x_ref, 3)
    r = ur + xr * mu[0:1]
    k = uk + xk * mu[1:2]
    v = uv + xv * mu[2:3]
    xw = ux + xx * mu[3:4]
    xa = ux + xx * mu[4:5]
    xg = ux + xx * mu[5:6]
    kk = k * vec[0:1]
    ss = _dot_hi(kk * kk, bd_ref[...])
    kk = kk / jnp.maximum(jnp.sqrt(ss), 1e-12)
    ro_ref[...] = r
    vo_ref[...] = v
    kko_ref[...] = kk
    go_ref[...] = _dot(_sigmoid(_dot(xg, g1_ref[...])), g2_ref[...])
    w_raw = _dot(jnp.tanh(_dot(xw, w1_ref[...])), w2_ref[...])
    a_raw = _dot(_dot(xa, a1_ref[...]), a2_ref[...])
    k_a = vec[1:2]
    for d, (wl_ref, kt_ref, ao_ref) in enumerate(((wl0_ref, kt0_ref, a0o_ref),
                                                   (wl1_ref, kt1_ref, a1o_ref))):
        w_d = vec[2 + d:3 + d] + w_raw[:, d * BR:(d + 1) * BR]
        a_d = _sigmoid(vec[4 + d:5 + d] + a_raw[:, d * BR:(d + 1) * BR])
        wl_ref[...] = -jnp.exp(-_softplus(-w_d) - 0.5)
        kt_ref[...] = k * (1.0 + (a_d - 1.0) * k_a)
        ao_ref[...] = a_d


def _dot3(a, b):
    a_hi = a.astype(BF16)
    b_hi = b.astype(BF16)
    a_lo = (a - a_hi.astype(F32)).astype(BF16)
    b_lo = (b - b_hi.astype(F32)).astype(BF16)
    d = lambda p, q: jnp.dot(p, q, preferred_element_type=F32)
    return d(a_hi, b_hi) + (d(a_hi, b_lo) + d(a_lo, b_hi))


TRI_BASE = 2
RW_GROUP = 4
RW_GW = RW_GROUP * RW_HD


def _bdiag(x):
    n = x.shape[0]
    tiled = jnp.concatenate([x] * RW_GROUP, axis=0)
    ri = lax.broadcasted_iota(jnp.int32, tiled.shape, 0)
    ci = lax.broadcasted_iota(jnp.int32, tiled.shape, 1)
    blk = x.shape[1] // RW_GROUP
    same_block = (ri >> int(math.log2(n))) == (ci >> int(math.log2(blk)))
    return jnp.where(same_block, tiled, jnp.zeros_like(tiled))


def _mm_cat(a_cat, b_cat, dot=_dot):
    return dot(a_cat, _bdiag(b_cat))


def _tri_inverse_cat(l_cat, c):
    ri = lax.broadcasted_iota(jnp.int32, l_cat.shape, 0)
    ci = lax.broadcasted_iota(jnp.int32, l_cat.shape, 1) & (c - 1)
    same = lambda b: (ri >> int(math.log2(b))) == (ci >> int(math.log2(b)))
    p = jnp.where(same(TRI_BASE), l_cat, 0.0)
    inv = (ri == ci).astype(F32) - p
    n = 2
    while n < TRI_BASE:
        p = _mm_cat(p, p, _dot3)
        inv = inv + _mm_cat(inv, p, _dot3)
        n *= 2
    b = TRI_BASE
    while b < c:
        off = jnp.where(jnp.logical_and(same(2 * b), jnp.logical_not(same(b))), l_cat, 0.0)
        inv = inv - _mm_cat(_mm_cat(inv, off), inv)
        b *= 2
    return inv


def _rw_chunk(r, wl, kt, v, kk, a, cum, st_ref, rev):
    c = r.shape[0]
    assert c == RW_HD
    lc = _dot_hi(cum, wl)
    e_neg = jnp.exp(-lc)
    alpha = kk * jnp.exp(lc - wl)
    beta = a * kk * e_neg
    gamma = kt * e_neg
    rho = r * jnp.exp(lc)
    p_end = jnp.exp(lc[0:1] if rev else lc[c - 1:c])
    ri = lax.broadcasted_iota(jnp.int32, (c, RW_GW), 0)
    ci = lax.broadcasted_iota(jnp.int32, (c, RW_GW), 1) & (c - 1)
    strict = (ci > ri) if rev else (ci < ri)
    incl = (ci >= ri) if rev else (ci <= ri)
    bi = lax.broadcasted_iota(jnp.int32, (RW_GW, RW_GW), 0) >> int(math.log2(RW_HD))
    bj = lax.broadcasted_iota(jnp.int32, (RW_GW, RW_GW), 1) >> int(math.log2(RW_HD))
    diag_blocks = bi == bj
    ys = []
    for grp in range(RW_HEADS // RW_GROUP):
        sl = slice(grp * RW_GW, (grp + 1) * RW_GW)
        al, be, ga, rh, vv = alpha[:, sl], beta[:, sl], gamma[:, sl], rho[:, sl], v[:, sl]
        st = st_ref[grp]
        gm = _dot_nt(jnp.concatenate([al, rh], axis=0),
                     jnp.concatenate([_bdiag(be), _bdiag(ga)], axis=0))
        l_b = jnp.where(strict, gm[:c, :RW_GW], 0.0)
        l_g = jnp.where(strict, gm[:c, RW_GW:], 0.0)
        r_b = jnp.where(incl, gm[c:, :RW_GW], 0.0)
        r_g = jnp.where(incl, gm[c:, RW_GW:], 0.0)
        t_inv = _tri_inverse_cat(l_b, c)
        w = _mm_cat(t_inv, al)
        u = _dot_nt(w, st) + _mm_cat(t_inv, _mm_cat(l_g, vv))
        ys.append(_dot_nt(rh, st) + _mm_cat(r_g, vv) - _mm_cat(r_b, u))
        upd = _dot_tn(vv, ga) - _dot_tn(u, be)
        st_ref[grp] = (st + jnp.where(diag_blocks, upd, 0.0)) * p_end[:, sl]
    return jnp.concatenate(ys, axis=-1)


def _rw_scan_body(rf_ref, wlf_ref, ktf_ref, vf_ref, kkf_ref, af_ref,
                  rb_ref, wlb_ref, ktb_ref, vb_ref, kkb_ref, ab_ref, cum_ref,
                  yf_ref, yb_ref, sf_ref, sb_ref):
    @pl.when(pl.program_id(0) == 0)
    def _init():
        sf_ref[...] = jnp.zeros_like(sf_ref)
        sb_ref[...] = jnp.zeros_like(sb_ref)

    yf_ref[...] = _rw_chunk(rf_ref[...], wlf_ref[...], ktf_ref[...], vf_ref[...], kkf_ref[...],
                            af_ref[...], cum_ref[0], sf_ref, False)
    yb_ref[...] = _rw_chunk(rb_ref[...], wlb_ref[...], ktb_ref[...], vb_ref[...], kkb_ref[...],
                            ab_ref[...], cum_ref[1], sb_ref, True)


def _rw_post_body(yf_ref, yb_ref, r_ref, kt0_ref, kt1_ref, v_ref, g_ref, vec_ref, bd_ref, o_ref):
    y = yf_ref[...] + yb_ref[...]
    bd = bd_ref[...]
    vec = vec_ref[...]
    mean = _dot_hi(y, bd) * (1.0 / RW_HD)
    yc = y - mean
    var = _dot_hi(yc * yc, bd) * (1.0 / RW_HD)
    yn = yc * lax.rsqrt(var + RW_GN_EPS) * vec[0:1] + vec[1:2]
    bonus = _dot_hi(r_ref[...] * (kt0_ref[...] + kt1_ref[...]) * vec[2:3], bd) * v_ref[...]
    o_ref[...] = ((yn + bonus) * g_ref[...]).astype(o_ref.dtype)


def _rwkv7_mixer(cols, n_ctx, mu, w0, w1, w2, a0, a1, a2, g1, g2, k_k, k_a, r_k, ln_g, ln_b):
    S = cols.shape[0]
    tb = ROW_BLOCK
    nblk = S // tb
    halo = _halo_rows(cols[:, COL_B * BR:(COL_B + 4) * BR], n_ctx, tb, 1, 1)
    mu8 = jnp.zeros((8, BR), F32).at[:6].set(mu)
    vec = jnp.zeros((8, BR), F32).at[0].set(k_k).at[1].set(k_a).at[2:4].set(w0).at[4:6].set(a0)
    w1c = jnp.concatenate([w1[0], w1[1]], axis=1).astype(BF16)
    a1c = jnp.concatenate([a1[0], a1[1]], axis=1).astype(BF16)
    w2c = _block_diag(w2).astype(BF16)
    a2c = _block_diag(a2).astype(BF16)
    bd = jnp.kron(jnp.eye(RW_HEADS, dtype=F32), jnp.ones((RW_HD, RW_HD), F32))
    row = lambda col: pl.BlockSpec((tb, BR), lambda i: (i, col))
    full = lambda arr: pl.BlockSpec(arr.shape, lambda i: (0,) * arr.ndim)
    consts = [mu8, vec, w1c, w2c, a1c, a2c, g1.astype(BF16), g2.astype(BF16), bd]
    outs = pl.pallas_call(
        _rw_prep_body,
        grid=(nblk,),
        in_specs=[row(COL_B), row(COL_B + 1), row(COL_B + 2), row(COL_B + 3),
                  pl.BlockSpec((None, 8, 4 * BR), lambda i: (i, 0, 0))] + [full(x) for x in consts],
        out_specs=[row(0)] * 10,
        out_shape=[jax.ShapeDtypeStruct((S, BR), F32)] * 10,
        compiler_params=_cparams(1),
    )(cols, cols, cols, cols, halo, *consts)
    r, v, kk, g, wl0, wl1, kt0, kt1, av0, av1 = outs

    c = CHUNK
    nch, ncc = S // c, n_ctx // c
    t = np.arange(c)
    cum = jnp.asarray(np.stack([(t[None, :] <= t[:, None]), (t[None, :] >= t[:, None])]).astype(np.float32))
    fwd = pl.BlockSpec((c, BR), lambda i: (i, 0))
    bwd = pl.BlockSpec((c, BR), lambda i: (_rev_block(i, ncc, nch), 0))
    y_f, y_b = pl.pallas_call(
        _rw_scan_body,
        grid=(nch,),
        in_specs=[fwd] * 6 + [bwd] * 6 + [pl.BlockSpec((2, c, c), lambda i: (0, 0, 0))],
        out_specs=[fwd, bwd],
        out_shape=[jax.ShapeDtypeStruct((S, BR), F32)] * 2,
        scratch_shapes=[pltpu.VMEM((RW_HEADS // RW_GROUP, RW_GW, RW_GW), F32)] * 2,
        compiler_params=_cparams(1),
    )(r, wl0, kt0, v, kk, av0, r, wl1, kt1, v, kk, av1, cum)

    pvec = jnp.zeros((8, BR), F32).at[0].set(ln_g).at[1].set(ln_b).at[2].set(r_k.reshape(-1))
    return pl.pallas_call(
        _rw_post_body,
        grid=(nblk,),
        in_specs=[row(0)] * 7 + [full(pvec), full(bd)],
        out_specs=row(0),
        out_shape=jax.ShapeDtypeStruct((S, BR), BF16),
        compiler_params=_cparams(1),
    )(y_f, y_b, r, kt0, kt1, v, g, pvec, bd)


def _merge_body(ya_ref, yb_ref, yc_ref, yd_ref, gate_ref, wb_ref, o_ref):
    d = o_ref.shape[1]
    acc = None
    for k, y_ref in enumerate((ya_ref, yb_ref, yc_ref, yd_ref)):
        p = jnp.dot(y_ref[...], wb_ref[k], preferred_element_type=F32)
        p = gate_ref[:, k * d:(k + 1) * d].astype(F32) * p
        acc = p if acc is None else acc + p
    o_ref[...] = acc.astype(o_ref.dtype)


def _merge(branches, gates, wb):
    S = gates.shape[0]
    d = wb.shape[2]
    tm = ROW_BLOCK
    row = pl.BlockSpec((tm, BR), lambda i: (i, 0))
    return pl.pallas_call(
        _merge_body,
        grid=(S // tm,),
        in_specs=[row] * 4 + [pl.BlockSpec((tm, N_BRANCH * d), lambda i: (i, 0)),
                              pl.BlockSpec(wb.shape, lambda i: (0, 0, 0))],
        out_specs=pl.BlockSpec((tm, d), lambda i: (i, 0)),
        out_shape=jax.ShapeDtypeStruct((S, d), BF16),
        compiler_params=_cparams(1),
    )(*branches, gates, wb)


def _expert_body(blk_e_ref, n_used_ref, x_ref, wg_ref, wu_ref, wd_ref, o_ref, wgb_ref, wub_ref, wdb_ref):
    b = pl.program_id(0)
    prev_e = blk_e_ref[jnp.maximum(b - 1, 0)]
    new_expert = jnp.logical_or(b == 0, blk_e_ref[b] != prev_e)
    used = b < n_used_ref[0]

    @pl.when(jnp.logical_and(used, new_expert))
    def _cast_weights():
        wgb_ref[...] = wg_ref[...].astype(BF16)
        wub_ref[...] = wu_ref[...].astype(BF16)
        wdb_ref[...] = wd_ref[...].astype(BF16)

    @pl.when(used)
    def _compute():
        x = x_ref[...]
        hg = jnp.dot(x, wgb_ref[...], preferred_element_type=F32)
        hu = jnp.dot(x, wub_ref[...], preferred_element_type=F32)
        hidden = (_silu(hg) * hu).astype(BF16)
        o_ref[...] = jnp.dot(hidden, wdb_ref[...], preferred_element_type=F32)

    @pl.when(jnp.logical_not(used))
    def _unused():
        o_ref[...] = jnp.zeros_like(o_ref)


def _expert_ffn(buf, blk_e, n_used, w_gate, w_up, w_down, layer):
    n_rows, D = buf.shape
    n_blocks = n_rows // MOE_BLOCK
    de = w_gate.shape[-1]
    grid_spec = pltpu.PrefetchScalarGridSpec(
        num_scalar_prefetch=2,
        grid=(n_blocks,),
        in_specs=[pl.BlockSpec((MOE_BLOCK, D), lambda b, be, nu: (b, 0)),
                  pl.BlockSpec((None, None, D, de), lambda b, be, nu: (layer, be[b], 0, 0)),
                  pl.BlockSpec((None, None, D, de), lambda b, be, nu: (layer, be[b], 0, 0)),
                  pl.BlockSpec((None, None, de, D), lambda b, be, nu: (layer, be[b], 0, 0))],
        out_specs=pl.BlockSpec((MOE_BLOCK, D), lambda b, be, nu: (b, 0)),
        scratch_shapes=[pltpu.VMEM((D, de), BF16), pltpu.VMEM((D, de), BF16), pltpu.VMEM((de, D), BF16)],
    )
    return pl.pallas_call(
        _expert_body,
        grid_spec=grid_spec,
        out_shape=jax.ShapeDtypeStruct((n_rows, D), F32),
        compiler_params=_cparams(1),
    )(blk_e, n_used, buf, w_gate, w_up, w_down)


def _combine_body(x_ref, y0_ref, y1_ref, w_ref, mod_ref, o_ref, *, gate_idx):
    w = w_ref[...]
    y = y0_ref[...] * w[:, 0:1] + y1_ref[...] * w[:, 1:2]
    o_ref[...] = x_ref[...] + mod_ref[gate_idx:gate_idx + 1] * y


def _route_body(lg_ref, bias_ref, tri_ref, ints_ref, wts_ref, cnt_out_ref, cnt_ref):
    @pl.when(pl.program_id(0) == 0)
    def _init():
        cnt_ref[...] = jnp.zeros_like(cnt_ref)

    scores = _sigmoid(lg_ref[...])
    sel = scores + bias_ref[...]
    tb = scores.shape[1]
    epg = EXPERTS_PER_GROUP
    row = lax.broadcasted_iota(jnp.int32, (epg, tb), 0)
    best = e0 = e1 = w0 = w1 = None
    for g in range(N_GROUPS):
        s = sel[g * epg:(g + 1) * epg]
        sc = scores[g * epg:(g + 1) * epg]
        m1 = jnp.max(s, axis=0, keepdims=True)
        i1 = jnp.min(jnp.where(s == m1, row, epg), axis=0, keepdims=True)
        first = row == i1
        s2 = jnp.where(first, -jnp.inf, s)
        m2 = jnp.max(s2, axis=0, keepdims=True)
        i2 = jnp.min(jnp.where(s2 == m2, row, epg), axis=0, keepdims=True)
        second = row == i2
        gs = m1 + m2
        a0 = jnp.sum(jnp.where(first, sc, 0.0), axis=0, keepdims=True)
        a1 = jnp.sum(jnp.where(second, sc, 0.0), axis=0, keepdims=True)
        if g == 0:
            best, e0, e1, w0, w1 = gs, i1, i2, a0, a1
        else:
            take = gs > best
            best = jnp.where(take, gs, best)
            e0 = jnp.where(take, i1 + g * epg, e0)
            e1 = jnp.where(take, i2 + g * epg, e1)
            w0 = jnp.where(take, a0, w0)
            w1 = jnp.where(take, a1, w1)
    rows_e = lax.broadcasted_iota(jnp.int32, scores.shape, 0)
    oh0 = rows_e == e0
    oh1 = rows_e == e1
    oh = jnp.logical_or(oh0, oh1).astype(F32)
    before = jnp.dot(oh.astype(BF16), tri_ref[...], preferred_element_type=F32) + cnt_ref[:, 0:1]
    rank0 = jnp.sum(jnp.where(oh0, before, 0.0), axis=0, keepdims=True)
    rank1 = jnp.sum(jnp.where(oh1, before, 0.0), axis=0, keepdims=True)
    cnt_ref[...] = cnt_ref[...] + jnp.sum(oh, axis=1, keepdims=True)
    cnt_out_ref[...] = cnt_ref[...]
    wsum = w0 + w1
    zi = jnp.zeros((4, tb), jnp.int32)
    ints_ref[...] = jnp.concatenate([e0, e1, rank0.astype(jnp.int32), rank1.astype(jnp.int32), zi], axis=0)
    wts_ref[...] = jnp.concatenate([w0 / wsum, w1 / wsum, jnp.zeros((6, tb), F32)], axis=0)


def _route(logits_t, router_b):
    E, n = logits_t.shape
    tb = ROW_BLOCK
    t = np.arange(tb)
    tri = jnp.asarray((t[:, None] < t[None, :]).astype(np.float32), BF16)
    ints, wts, cnt = pl.pallas_call(
        _route_body,
        grid=(n // tb,),
        in_specs=[pl.BlockSpec((E, tb), lambda i: (0, i)),
                  pl.BlockSpec((E, 1), lambda i: (0, 0)),
                  pl.BlockSpec((tb, tb), lambda i: (0, 0))],
        out_specs=[pl.BlockSpec((8, tb), lambda i: (0, i)),
                   pl.BlockSpec((8, tb), lambda i: (0, i)),
                   pl.BlockSpec((E, 128), lambda i: (0, 0))],
        out_shape=[jax.ShapeDtypeStruct((8, n), jnp.int32), jax.ShapeDtypeStruct((8, n), F32),
                   jax.ShapeDtypeStruct((E, 128), F32)],
        scratch_shapes=[pltpu.VMEM((E, 128), F32)],
        compiler_params=_cparams(1),
    )(logits_t, router_b.astype(F32)[:, None], tri)
    return ints, wts, cnt[:, 0].astype(jnp.int32)


def _moe(xres, h, logits_t, router_b, w_gate, w_up, w_down, layer, mod, n_ctx_blocks, row0):
    S, D = xres.shape
    hh = h[row0:]
    n = hh.shape[0]
    ints, wts, sizes = _route(logits_t[:, row0:], router_b)
    n_assign = n * TOP_K
    padded = ((sizes + MOE_BLOCK - 1) // MOE_BLOCK) * MOE_BLOCK
    pad_ends = jnp.cumsum(padded)
    pad_starts = pad_ends - padded
    dest0 = pad_starts[ints[0]] + ints[2]
    dest1 = pad_starts[ints[1]] + ints[3]
    n_blocks = -(-n_assign // MOE_BLOCK) + N_EXPERTS
    tok = jnp.arange(n, dtype=jnp.int32)
    src = jnp.zeros((n_blocks * MOE_BLOCK,), jnp.int32).at[dest0].set(tok).at[dest1].set(tok)
    buf = hh[src]
    blk_e = jnp.minimum(jnp.searchsorted(pad_ends, jnp.arange(n_blocks) * MOE_BLOCK, side="right"),
                        N_EXPERTS - 1).astype(jnp.int32)
    n_used = (pad_ends[-1] // MOE_BLOCK).astype(jnp.int32).reshape(1)
    yb = _expert_ffn(buf, blk_e, n_used, w_gate, w_up, w_down, layer)
    y0 = yb[dest0]
    y1 = yb[dest1]
    tm = ROW_BLOCK
    r0 = row0 // tm
    row = pl.BlockSpec((tm, D), lambda i: (i, 0))
    xrow = pl.BlockSpec((tm, D), lambda i: (i + r0, 0))
    out = pl.pallas_call(
        functools.partial(_combine_body, gate_idx=5),
        grid=(n // tm,),
        in_specs=[xrow, row, row, pl.BlockSpec((tm, 8), lambda i: (i, 0)),
                  pl.BlockSpec((None, 8, D), lambda i: (_row_class(i + r0, n_ctx_blocks), 0, 0))],
        out_specs=row,
        out_shape=jax.ShapeDtypeStruct((n, D), F32),
        compiler_params=_cparams(1),
    )(xres, y0, y1, wts.T, mod)
    if row0:
        out = jnp.concatenate([xres[:row0], out], axis=0)
    return out


def _grid_sincos(rows, d_model):
    nf = d_model // 4
    omega = POS_BASE ** (-jnp.arange(nf, dtype=F32) / nf)
    ang_r = jnp.arange(rows, dtype=F32)[:, None] * omega
    ang_c = jnp.arange(GRID_W, dtype=F32)[:, None] * omega
    f_r = jnp.concatenate([jnp.sin(ang_r), jnp.cos(ang_r)], axis=-1)[:, None, :]
    f_c = jnp.concatenate([jnp.sin(ang_c), jnp.cos(ang_c)], axis=-1)[None, :, :]
    shape = (rows, GRID_W, 2 * nf)
    pos = jnp.concatenate([jnp.broadcast_to(f_r, shape), jnp.broadcast_to(f_c, shape)], axis=-1)
    return pos.reshape(rows * GRID_W, d_model)


def kernel(x, c, ctx, c_ctx, w_mod, b_mod, norm_mix_g, norm_ffn_g, w_in, w_branch, w_out, rw_mu, rw_w0, rw_w1, rw_w2, rw_a0, rw_a1, rw_a2, rw_g1, rw_g2, rw_kk, rw_ka, rw_rk, rw_ln_g, rw_ln_b, hg_lb_logits, hg_norm_g, lru_conv_w, lru_conv_b, lru_wa, lru_ba, lru_wx, lru_bx, lru_lam, router_w, router_b, moe_w_gate, moe_w_up, moe_w_down, final_norm_g):
    _, T, D = x.shape
    n_ctx = ctx.shape[1]
    depth = w_in.shape[0]
    S = n_ctx + T
    ncb = n_ctx // ROW_BLOCK
    n_mix = N_COL_BLOCKS * BR

    xs = jnp.concatenate([ctx[0], x[0] + _grid_sincos(T // GRID_W, D)], axis=0)
    lb_cum = jnp.cumsum(jax.nn.softmax(hg_lb_logits.astype(F32), axis=0), axis=0)
    hg_lb = lb_cum - lb_cum[:1]
    cvec = jnp.zeros((8, D), F32).at[0].set(jax.nn.silu(c_ctx)).at[1].set(jax.nn.silu(c[0]))

    for l in range(depth):
        last = l == depth - 1
        mod = _mm(cvec, w_mod, layer=l, n_out=N_MOD * D, tm=8, tn=1024, out_dtype=F32,
                  bias=b_mod[l][None])
        mod = jnp.zeros((2, 8, D), F32).at[:, :N_MOD].set(mod[:2].reshape(2, N_MOD, D))
        h = _norm_mod(xs, norm_mix_g[l][None], mod, shift_idx=0, scale_idx=1, out_dtype=BF16,
                      n_ctx_blocks=ncb)
        tm = _pick_tm(S)
        cols = _mm(h, w_in, layer=l, n_out=n_mix, tm=tm, tn=1024, out_dtype=F32)
        gates = _mm(h, w_in, layer=l, col0=n_mix // 1024, n_out=N_BRANCH * D, tm=tm, tn=1024,
                    out_dtype=BF16, act="sigmoid")
        y_a = _fnet_mixer(cols, n_ctx)
        y_b = _rwkv7_mixer(cols, n_ctx, rw_mu[l], rw_w0[l], rw_w1[l], rw_w2[l], rw_a0[l], rw_a1[l],
                           rw_a2[l], rw_g1[l], rw_g2[l], rw_kk[l], rw_ka[l], rw_rk[l], rw_ln_g[l],
                           rw_ln_b[l])
        y_c = _hgrn2_mixer(cols, n_ctx, hg_lb[l], hg_norm_g[l])
        y_d = _rglru_mixer(cols, n_ctx, lru_conv_w[l], lru_conv_b[l], lru_wa[l], lru_ba[l],
                           lru_wx[l], lru_bx[l], lru_lam[l])
        merged = _merge((y_a, y_b, y_c, y_d), gates, w_branch[l].astype(BF16))
        xs = _mm(merged, w_out, layer=l, n_out=D, tm=ROW_BLOCK, tn=1024, out_dtype=F32,
                 res=xs, scale=mod[:, 2:3], n_ctx_blocks=ncb)
        h2, logits = _norm_mod(xs, norm_ffn_g[l][None], mod, shift_idx=3, scale_idx=4,
                               out_dtype=BF16, n_ctx_blocks=ncb, router_w=router_w)
        xs = _moe(xs, h2, logits, router_b, moe_w_gate, moe_w_up, moe_w_down, l, mod, ncb,
                  n_ctx if last else 0)
    out = _norm_mod(xs[n_ctx:], final_norm_g[None], jnp.zeros((2, 8, D), F32), shift_idx=None,
                    scale_idx=None, out_dtype=F32, n_ctx_blocks=0)
    return out[None]
```

```python
import functools
import math

import numpy as np
import jax
import jax.numpy as jnp
from jax import lax
from jax.experimental import pallas as pl
from jax.experimental.pallas import tpu as pltpu

F32 = jnp.float32
BF16 = jnp.bfloat16
HIGHEST = lax.Precision.HIGHEST

NORM_EPS = 1e-6
GRID_W = 64
POS_BASE = 10000.0
N_MOD = 6
BR = 512
N_BRANCH = 4
FN_GROUPS = 4
FN_GC = BR // FN_GROUPS
RW_HEADS = 8
RW_HD = BR // RW_HEADS
RW_GN_EPS = 64e-5
HG_HEADS = 4
HG_DK = BR // HG_HEADS
LRU_BLOCKS = 8
LRU_C = 8.0
COL_A, COL_B, COL_C, COL_D = 0, 1, 5, 10
N_COL_BLOCKS = 12
N_EXPERTS = 64
N_GROUPS = 8
EXPERTS_PER_GROUP = N_EXPERTS // N_GROUPS
TOP_K = 2
MOE_BLOCK = 128

ROW_BLOCK = 256
CHUNK = 64
FFT_N2 = 64
VMEM_LIMIT = 56 * 1024 * 1024
NEG_BIG = -1e30


def _cparams(n_axes):
    return pltpu.CompilerParams(dimension_semantics=("arbitrary",) * n_axes,
                                vmem_limit_bytes=VMEM_LIMIT)


def _rev_block(g, n_ctx_blocks, n_blocks):
    return jnp.where(g < n_ctx_blocks, n_ctx_blocks - 1 - g, n_blocks - 1 - (g - n_ctx_blocks))


def _row_class(i, n_ctx_blocks):
    return jnp.where(i < n_ctx_blocks, 0, 1)


def _sigmoid(x):
    return jax.nn.sigmoid(x)


def _silu(x):
    return x * jax.nn.sigmoid(x)


def _log_sigmoid(x):
    return jnp.minimum(x, 0.0) - jnp.log1p(jnp.exp(-jnp.abs(x)))


def _softplus(x):
    return jnp.maximum(x, 0.0) + jnp.log1p(jnp.exp(-jnp.abs(x)))


def _dot(a, b):
    return jnp.dot(a.astype(BF16), b.astype(BF16), preferred_element_type=F32)


def _dot_nt(a, b):
    return lax.dot_general(a.astype(BF16), b.astype(BF16), (((1,), (1,)), ((), ())),
                           preferred_element_type=F32)


def _dot_tn(a, b):
    return lax.dot_general(a.astype(BF16), b.astype(BF16), (((0,), (0,)), ((), ())),
                           preferred_element_type=F32)


def _dot_hi(a, b):
    return jnp.dot(a, b, precision=HIGHEST, preferred_element_type=F32)


def _mm_body(*refs, act, has_bias, has_res):
    a_ref, b_ref = refs[0], refs[1]
    k = 2
    bias_ref = res_ref = scale_ref = None
    if has_bias:
        bias_ref = refs[k]
        k += 1
    if has_res:
        res_ref, scale_ref = refs[k], refs[k + 1]
        k += 2
    o_ref, wb_ref = refs[k], refs[k + 1]

    @pl.when(pl.program_id(1) == 0)
    def _cast_weights():
        wb_ref[...] = b_ref[...].astype(BF16)

    acc = jnp.dot(a_ref[...].astype(BF16), wb_ref[...], preferred_element_type=F32)
    if has_bias:
        acc = acc + bias_ref[...]
    if act == "sigmoid":
        acc = _sigmoid(acc)
    if has_res:
        acc = res_ref[...] + scale_ref[...] * acc
    o_ref[...] = acc.astype(o_ref.dtype)


def _pick_tm(m):
    return next(t for t in (768, 512, 256, 128, 64) if m % t == 0)


def _mm(a, b, *, n_out, tm, tn, out_dtype, layer=None, col0=0, act=None, bias=None,
        res=None, scale=None, n_ctx_blocks=1):
    M, K = a.shape
    assert M % tm == 0 and n_out % tn == 0
    grid = (n_out // tn, M // tm)
    in_specs = [pl.BlockSpec((tm, K), lambda j, i: (i, 0))]
    if layer is None:
        in_specs.append(pl.BlockSpec((K, tn), lambda j, i: (0, j + col0)))
    else:
        in_specs.append(pl.BlockSpec((None, K, tn), lambda j, i: (layer, 0, j + col0)))
    args = [a, b]
    if bias is not None:
        in_specs.append(pl.BlockSpec((1, tn), lambda j, i: (0, j)))
        args.append(bias)
    if res is not None:
        in_specs.append(pl.BlockSpec((tm, tn), lambda j, i: (i, j)))
        in_specs.append(pl.BlockSpec((None, 1, tn), lambda j, i: (_row_class(i, n_ctx_blocks), 0, j)))
        args += [res, scale]
    return pl.pallas_call(
        functools.partial(_mm_body, act=act, has_bias=bias is not None, has_res=res is not None),
        grid=grid,
        in_specs=in_specs,
        out_specs=pl.BlockSpec((tm, tn), lambda j, i: (i, j)),
        out_shape=jax.ShapeDtypeStruct((M, n_out), out_dtype),
        scratch_shapes=[pltpu.VMEM((K, tn), BF16)],
        compiler_params=_cparams(2),
    )(*args)


def _norm_body(*refs, shift_idx, scale_idx, with_router):
    if with_router:
        x_ref, g_ref, mod_ref, rw_ref, o_ref, lg_ref = refs
    else:
        x_ref, g_ref, mod_ref, o_ref = refs
    x = x_ref[...]
    y = x * lax.rsqrt(jnp.mean(x * x, axis=-1, keepdims=True) + NORM_EPS) * g_ref[...]
    if scale_idx is not None:
        m = mod_ref[...]
        y = y * (1.0 + m[scale_idx:scale_idx + 1]) + m[shift_idx:shift_idx + 1]
    o_ref[...] = y.astype(o_ref.dtype)
    if with_router:
        lg_ref[...] = lax.dot_general(rw_ref[...], y, (((1,), (1,)), ((), ())), precision=HIGHEST,
                                      preferred_element_type=F32)


def _norm_mod(x, g, mod, *, shift_idx, scale_idx, out_dtype, n_ctx_blocks, router_w=None):
    S, D = x.shape
    tm = ROW_BLOCK
    in_specs = [pl.BlockSpec((tm, D), lambda i: (i, 0)),
                pl.BlockSpec((1, D), lambda i: (0, 0)),
                pl.BlockSpec((None, 8, D), lambda i: (_row_class(i, n_ctx_blocks), 0, 0))]
    args = [x, g, mod]
    out_specs = pl.BlockSpec((tm, D), lambda i: (i, 0))
    out_shape = jax.ShapeDtypeStruct((S, D), out_dtype)
    if router_w is not None:
        in_specs.append(pl.BlockSpec((N_EXPERTS, D), lambda i: (0, 0)))
        args.append(router_w.T)
        out_specs = [out_specs, pl.BlockSpec((N_EXPERTS, tm), lambda i: (0, i))]
        out_shape = [out_shape, jax.ShapeDtypeStruct((N_EXPERTS, S), F32)]
    return pl.pallas_call(
        functools.partial(_norm_body, shift_idx=shift_idx, scale_idx=scale_idx,
                          with_router=router_w is not None),
        grid=(S // tm,),
        in_specs=in_specs,
        out_specs=out_specs,
        out_shape=out_shape,
        compiler_params=_cparams(1),
    )(*args)


def _dft_tables(n, scale):
    k = jnp.arange(n, dtype=jnp.int32)
    ang = ((k[:, None] * k[None, :]) % n).astype(F32) * (2.0 * math.pi / n)
    return jnp.cos(ang) * scale, jnp.sin(ang) * scale


def _channel_dft_tables():
    c, s = _dft_tables(FN_GC, FN_GC ** -0.5)
    eye = jnp.eye(FN_GROUPS, dtype=F32)
    return jnp.kron(eye, c), jnp.kron(eye, s)


def _fnet_dense_body(u_ref, cc_ref, sc_ref, cl_ref, sl_ref, o_ref):
    u = u_ref[...]
    ac = _dot_hi(u, cc_ref[...])
    as_ = _dot_hi(u, sc_ref[...])
    o_ref[...] = _dot_hi(cl_ref[...], ac) - _dot_hi(sl_ref[...], as_)


def _fnet_dense(u, cc, sc):
    L = u.shape[0]
    cl, sl = _dft_tables(L, L ** -0.5)
    return pl.pallas_call(
        _fnet_dense_body,
        out_shape=jax.ShapeDtypeStruct((L, BR), F32),
        compiler_params=pltpu.CompilerParams(vmem_limit_bytes=VMEM_LIMIT),
    )(u, cc, sc, cl, sl)


def _fnet_stage1_body(u_ref, gr_ref, gi_ref, cc_ref, sc_ref, yr_ref, yi_ref):
    u = u_ref[...]
    zr = _dot_hi(u, cc_ref[...])
    zi = -_dot_hi(u, sc_ref[...])
    gr, gi = gr_ref[...], gi_ref[...]
    yr_ref[...] = _dot_hi(gr, zr) - _dot_hi(gi, zi)
    yi_ref[...] = _dot_hi(gr, zi) + _dot_hi(gi, zr)


def _fnet_stage2_body(yr_ref, yi_ref, c2_ref, s2_ref, o_ref):
    c2, s2 = c2_ref[...], s2_ref[...]
    for j in range(8):
        o_ref[:, j, :] = _dot_hi(c2, yr_ref[j]) + _dot_hi(s2, yi_ref[j])


def _fnet_two_stage(u, cc, sc):
    L = u.shape[0]
    n2 = FFT_N2
    n1 = L // n2
    assert n1 * n2 == L and n1 % 8 == 0
    f1 = jnp.arange(n1, dtype=jnp.int32)
    t1 = jnp.arange(n1, dtype=jnp.int32)
    t2 = jnp.arange(n2, dtype=jnp.int32)
    t = n2 * t1[None, None, :] + t2[:, None, None]
    ang = ((f1[None, :, None] * t) % L).astype(F32) * (2.0 * math.pi / L)
    gr = jnp.cos(ang) * (L ** -0.5)
    gi = -jnp.sin(ang) * (L ** -0.5)
    c2, s2 = _dft_tables(n2, 1.0)
    u2 = u.reshape(n1, n2 * BR)
    yr, yi = pl.pallas_call(
        _fnet_stage1_body,
        grid=(n2,),
        in_specs=[pl.BlockSpec((n1, BR), lambda j: (0, j)),
                  pl.BlockSpec((None, n1, n1), lambda j: (j, 0, 0)),
                  pl.BlockSpec((None, n1, n1), lambda j: (j, 0, 0)),
                  pl.BlockSpec((BR, BR), lambda j: (0, 0)),
                  pl.BlockSpec((BR, BR), lambda j: (0, 0))],
        out_specs=[pl.BlockSpec((n1, BR), lambda j: (0, j)),
                   pl.BlockSpec((n1, BR), lambda j: (0, j))],
        out_shape=[jax.ShapeDtypeStruct((n1, n2 * BR), F32)] * 2,
        compiler_params=_cparams(1),
    )(u2, gr, gi, cc, sc)
    out = pl.pallas_call(
        _fnet_stage2_body,
        grid=(n1 // 8,),
        in_specs=[pl.BlockSpec((8, n2, BR), lambda i: (i, 0, 0)),
                  pl.BlockSpec((8, n2, BR), lambda i: (i, 0, 0)),
                  pl.BlockSpec((n2, n2), lambda i: (0, 0)),
                  pl.BlockSpec((n2, n2), lambda i: (0, 0))],
        out_specs=pl.BlockSpec((n2, 8, BR), lambda i: (0, i, 0)),
        out_shape=jax.ShapeDtypeStruct((n2, n1, BR), F32),
        compiler_params=_cparams(1),
    )(yr.reshape(n1, n2, BR), yi.reshape(n1, n2, BR), c2, s2)
    return out.reshape(L, BR)


def _fnet_mixer(cols, n_ctx):
    cc, sc = _channel_dft_tables()
    ua = cols[:, COL_A * BR:(COL_A + 1) * BR]
    y_ctx = _fnet_dense(ua[:n_ctx], cc, sc)
    y_lat = _fnet_two_stage(ua[n_ctx:], cc, sc)
    return jnp.concatenate([y_ctx, y_lat], axis=0).astype(BF16)


def _halo_rows(x, c0, c1, n_ctx, tb, before, after):
    C = c1 - c0
    blocks = x.reshape(-1, tb, x.shape[1])
    ncb = n_ctx // tb
    parts = []
    for blk in (blocks[:ncb], blocks[ncb:]):
        prev = jnp.concatenate([jnp.zeros((1, before, C), x.dtype), blk[:-1, tb - before:, c0:c1]], axis=0)
        nxt = jnp.concatenate([blk[1:, :after, c0:c1], jnp.zeros((1, after, C), x.dtype)], axis=0)
        pad = jnp.zeros((blk.shape[0], 8 - before - after, C), x.dtype)
        parts.append(jnp.concatenate([prev, nxt, pad], axis=1))
    return jnp.concatenate(parts, axis=0)


def _shift_rows(x, halo_row, down):
    n = x.shape[0]
    row = lax.broadcasted_iota(jnp.int32, (n, 1), 0)
    if down:
        return jnp.where(row == 0, halo_row, pltpu.roll(x, 1, axis=0))
    return jnp.where(row == n - 1, halo_row, pltpu.roll(x, n - 1, axis=0))


def _lru_prep_body(x_ref, halo_ref, cw_ref, cb_ref, w4_ref, b4_ref, lam_ref,
                   af_ref, bf_ref, ab_ref, bb_ref):
    x = x_ref[...]
    halo = halo_ref[...]
    x_m1 = _shift_rows(x, halo[1:2], True)
    x_m2 = _shift_rows(x_m1, halo[0:1], True)
    x_p1 = _shift_rows(x, halo[2:3], False)
    cw = cw_ref[...]
    xc = cb_ref[...] + x_m2 * cw[0:1] + x_m1 * cw[1:2] + x * cw[2:3] + x_p1 * cw[3:4]
    pre = _sigmoid(_dot(xc, w4_ref[...]) + b4_ref[...])
    sp = _softplus(-lam_ref[...])
    for d, (a_ref, b_ref) in enumerate(((af_ref, bf_ref), (ab_ref, bb_ref))):
        r = pre[:, d * BR:(d + 1) * BR]
        i = pre[:, (2 + d) * BR:(3 + d) * BR]
        log_a = (-LRU_C) * r * sp[d:d + 1]
        a_ref[...] = jnp.exp(log_a)
        b_ref[...] = jnp.sqrt(jnp.maximum(1.0 - jnp.exp(2.0 * log_a), 0.0)) * i * xc


def _lru_scan_body(af_ref, bf_ref, ab_ref, bb_ref, hf_ref, hb_ref, sf_ref, sb_ref):
    @pl.when(pl.program_id(0) == 0)
    def _init():
        sf_ref[...] = jnp.zeros_like(sf_ref)
        sb_ref[...] = jnp.zeros_like(sb_ref)

    tb = af_ref.shape[0]

    def step(t, carry):
        hf, hb = carry
        hf = af_ref[pl.ds(t, 1), :] * hf + bf_ref[pl.ds(t, 1), :]
        hf_ref[pl.ds(t, 1), :] = hf
        u = tb - 1 - t
        hb = ab_ref[pl.ds(u, 1), :] * hb + bb_ref[pl.ds(u, 1), :]
        hb_ref[pl.ds(u, 1), :] = hb
        return hf, hb

    hf, hb = lax.fori_loop(0, tb, step, (sf_ref[...], sb_ref[...]), unroll=8)
    sf_ref[...] = hf
    sb_ref[...] = hb


def _gelu_tanh(x):
    return 0.5 * x * (1.0 + jnp.tanh(math.sqrt(2.0 / math.pi) * (x + 0.044715 * (x * x * x))))


def _lru_post_body(hf_ref, hb_ref, g_ref, o_ref):
    o_ref[...] = ((hf_ref[...] + hb_ref[...]) * _gelu_tanh(g_ref[...])).astype(o_ref.dtype)


def _block_diag(w):
    G, n, m = w.shape
    eye = jnp.eye(G, dtype=w.dtype)
    return (eye[:, None, :, None] * w[:, :, None, :]).reshape(G * n, G * m)


def _rglru_mixer(cols, n_ctx, conv_w, conv_b, wa, ba, wx, bx, lam):
    S = cols.shape[0]
    tb = ROW_BLOCK
    nblk, ncb = S // tb, n_ctx // tb
    halo = _halo_rows(cols, COL_D * BR, (COL_D + 1) * BR, n_ctx, tb, 2, 1)
    w4 = jnp.concatenate([_block_diag(wa[0]), _block_diag(wa[1]),
                          _block_diag(wx[0]), _block_diag(wx[1])], axis=1).astype(BF16)
    b4 = jnp.concatenate([ba[0], ba[1], bx[0], bx[1]])[None]
    cw8 = jnp.zeros((8, BR), F32).at[:4].set(conv_w)
    lam8 = jnp.zeros((8, BR), F32).at[:2].set(lam)
    row = lambda c: pl.BlockSpec((tb, BR), lambda i: (i, c))
    full = lambda shp: pl.BlockSpec(shp, lambda i: (0,) * len(shp))
    a_f, b_f, a_b, b_b = pl.pallas_call(
        _lru_prep_body,
        grid=(nblk,),
        in_specs=[row(COL_D), pl.BlockSpec((None, 8, BR), lambda i: (i, 0, 0)),
                  full((8, BR)), full((1, BR)), full((BR, 4 * BR)), full((1, 4 * BR)), full((8, BR))],
        out_specs=[row(0)] * 4,
        out_shape=[jax.ShapeDtypeStruct((S, BR), F32)] * 4,
        compiler_params=_cparams(1),
    )(cols, halo, cw8, conv_b[None], w4, b4, lam8)
    fwd = pl.BlockSpec((tb, BR), lambda g: (g, 0))
    bwd = pl.BlockSpec((tb, BR), lambda g: (_rev_block(g, ncb, nblk), 0))
    h_f, h_b = pl.pallas_call(
        _lru_scan_body,
        grid=(nblk,),
        in_specs=[fwd, fwd, bwd, bwd],
        out_specs=[fwd, bwd],
        out_shape=[jax.ShapeDtypeStruct((S, BR), F32)] * 2,
        scratch_shapes=[pltpu.VMEM((1, BR), F32), pltpu.VMEM((1, BR), F32)],
        compiler_params=_cparams(1),
    )(a_f, b_f, a_b, b_b)
    return pl.pallas_call(
        _lru_post_body,
        grid=(nblk,),
        in_specs=[row(0), row(0), row(COL_D + 1)],
        out_specs=row(0),
        out_shape=jax.ShapeDtypeStruct((S, BR), BF16),
        compiler_params=_cparams(1),
    )(h_f, h_b, cols)


def _level_sizes(c):
    out, b = [], c // 2
    while b >= 1:
        out.append(b)
        b //= 2
    return out


def _hgrn_tables(c):
    t = np.arange(c)
    sizes = _level_sizes(c)
    mats = np.zeros((2, (1 + len(sizes)) * c, c), np.float32)
    masks = np.zeros((len(sizes), c, c), np.float32)
    cum = [(t[None, :] <= t[:, None]).astype(np.float32), (t[None, :] >= t[:, None]).astype(np.float32)]
    for d in range(2):
        mats[d, :c] = cum[d]
        for l, b in enumerate(sizes):
            ref = (t // (2 * b)) * (2 * b) + (b - 1 if d == 0 else b)
            mats[d, (l + 1) * c:(l + 2) * c] = cum[d] - cum[d][ref]
    for l, b in enumerate(sizes):
        masks[l] = (t[:, None] // (2 * b) == t[None, :] // (2 * b)).astype(np.float32)
    return jnp.asarray(mats), jnp.asarray(masks)


def _hgrn_prepare(chunks, lb_ref, masks_ref):
    c = chunks[0][0].shape[0]
    sizes = _level_sizes(c)
    log_lb, log_1mlb, one_m_lb = lb_ref[0:1], lb_ref[1:2], lb_ref[2:3]
    row = lax.broadcasted_iota(jnp.int32, (c, 1), 0)
    eye = (lax.broadcasted_iota(jnp.int32, (c, c), 0) == lax.broadcasted_iota(jnp.int32, (c, c), 1))
    items = []
    for q_raw, z, v, mats, rev in chunks:
        q = _silu(q_raw)
        a = log_lb
        b = log_1mlb + _log_sigmoid(z)
        lf = jnp.maximum(a, b) + jnp.log1p(jnp.exp(-jnp.abs(a - b)))
        k = one_m_lb * _sigmoid(-z)
        dall = _dot_hi(mats, lf)
        g = dall[:c]
        g_last = g[0:1] if rev else g[c - 1:c]
        qg = (q * jnp.exp(g)).astype(BF16)
        kg = (k * jnp.exp(g_last - g)).astype(BF16)
        dec = jnp.exp(g_last)
        q_lv, k_lv = [], []
        for l, b in enumerate(sizes):
            dl = dall[(l + 1) * c:(l + 2) * c]
            second = ((row >> int(math.log2(b))) & 1) == 1
            q_side = jnp.logical_not(second) if rev else second
            k_side = jnp.logical_not(q_side)
            q_lv.append((q * jnp.exp(jnp.where(q_side, dl, NEG_BIG))).astype(BF16))
            k_lv.append((k * jnp.exp(jnp.where(k_side, -dl, NEG_BIG))).astype(BF16))
        qk = q * k
        vb = v.astype(BF16)
        for h in range(HG_HEADS):
            sl = slice(h * HG_DK, (h + 1) * HG_DK)
            items.append(dict(qg=qg[:, sl], kg=kg[:, sl], dec=dec[:, sl], v=vb[:, sl],
                              diag=jnp.sum(qk[:, sl], axis=-1, keepdims=True),
                              q_lv=[x[:, sl] for x in q_lv], k_lv=[x[:, sl] for x in k_lv]))
    prods = [[_dot_nt(it["q_lv"][l], it["k_lv"][l]) for it in items] for l in range(len(sizes))]
    scores = []
    for i, it in enumerate(items):
        s = jnp.where(eye, it["diag"], 0.0)
        for l in range(len(sizes)):
            s = s + masks_ref[l] * prods[l][i]
        scores.append(s)
    y_intra = [_dot(s, it["v"]) for s, it in zip(scores, items)]
    kv = [_dot_tn(it["v"], it["kg"]) for it in items]
    out = [dict(qg=it["qg"], dec=it["dec"], y_intra=y, kv=x) for it, y, x in zip(items, y_intra, kv)]
    return [out[i * HG_HEADS:(i + 1) * HG_HEADS] for i in range(len(chunks))]


def _hgrn_apply(problems):
    items = [(g, st_ref, h) for heads, st_ref in problems for h, g in enumerate(heads)]
    sts = [st_ref[h] for _, st_ref, h in items]
    ys = [_dot_nt(g["qg"], st) + g["y_intra"] for (g, _, _), st in zip(items, sts)]
    for (g, st_ref, h), st in zip(items, sts):
        st_ref[h] = st * g["dec"] + g["kv"]
    return [jnp.concatenate(ys[i * HG_HEADS:(i + 1) * HG_HEADS], axis=-1) for i in range(len(problems))]


HG_SUB = 4


def _hgrn_scan_body(qf_ref, zf_ref, vf_ref, qb_ref, zb_ref, vb_ref, lb_ref, mats_ref, masks_ref,
                    yf_ref, yb_ref, sf_ref, sb_ref):
    @pl.when(pl.program_id(0) == 0)
    def _init():
        sf_ref[...] = jnp.zeros_like(sf_ref)
        sb_ref[...] = jnp.zeros_like(sb_ref)

    c = CHUNK
    rows = lambda j: pl.ds(j * c, c)
    chunks = [(qf_ref[rows(j), :], zf_ref[rows(j), :], vf_ref[rows(j), :], mats_ref[0], False)
              for j in range(HG_SUB)]
    chunks += [(qb_ref[rows(j), :], zb_ref[rows(j), :], vb_ref[rows(j), :], mats_ref[1], True)
               for j in range(HG_SUB)]
    prep = _hgrn_prepare(chunks, lb_ref, masks_ref)
    for j in range(HG_SUB):
        jb = HG_SUB - 1 - j
        y_f, y_b = _hgrn_apply([(prep[j], sf_ref), (prep[HG_SUB + jb], sb_ref)])
        yf_ref[rows(j), :] = y_f
        yb_ref[rows(jb), :] = y_b


def _hgrn_post_body(yf_ref, yb_ref, og_ref, ng_ref, o_ref):
    y = yf_ref[...] + yb_ref[...]
    outs = []
    for h in range(HG_HEADS):
        yh = y[:, h * HG_DK:(h + 1) * HG_DK]
        outs.append(yh * lax.rsqrt(jnp.mean(yh * yh, axis=-1, keepdims=True) + NORM_EPS) * ng_ref[...])
    o_ref[...] = (jnp.concatenate(outs, axis=-1) * _silu(og_ref[...])).astype(o_ref.dtype)


def _hgrn2_mixer(cols, n_ctx, lb, norm_g):
    S = cols.shape[0]
    rows = HG_SUB * CHUNK
    nch, ncc = S // rows, n_ctx // rows
    mats, masks = _hgrn_tables(CHUNK)
    lb8 = jnp.zeros((8, BR), F32).at[0].set(jnp.log(lb)).at[1].set(jnp.log1p(-lb)).at[2].set(1.0 - lb)
    fwd = lambda col: pl.BlockSpec((rows, BR), lambda g: (g, col))
    bwd = lambda col: pl.BlockSpec((rows, BR), lambda g: (_rev_block(g, ncc, nch), col))
    full = lambda shp: pl.BlockSpec(shp, lambda g: (0,) * len(shp))
    y_f, y_b = pl.pallas_call(
        _hgrn_scan_body,
        grid=(nch,),
        in_specs=[fwd(COL_C), fwd(COL_C + 1), fwd(COL_C + 3),
                  bwd(COL_C), bwd(COL_C + 2), bwd(COL_C + 3),
                  full((8, BR)), full(mats.shape), full(masks.shape)],
        out_specs=[fwd(0), bwd(0)],
        out_shape=[jax.ShapeDtypeStruct((S, BR), F32)] * 2,
        scratch_shapes=[pltpu.VMEM((HG_HEADS, HG_DK, HG_DK), F32)] * 2,
        compiler_params=_cparams(1),
    )(cols, cols, cols, cols, cols, cols, lb8, mats, masks)
    tb = ROW_BLOCK
    row = lambda col: pl.BlockSpec((tb, BR), lambda i: (i, col))
    return pl.pallas_call(
        _hgrn_post_body,
        grid=(S // tb,),
        in_specs=[row(0), row(0), row(COL_C + 4), pl.BlockSpec((1, HG_DK), lambda i: (0, 0))],
        out_specs=row(0),
        out_shape=jax.ShapeDtypeStruct((S, BR), BF16),
        compiler_params=_cparams(1),
    )(y_f, y_b, cols, norm_g[None])


def _rw_prep_body(r_ref, k_ref, v_ref, x_ref, halo_ref, mu_ref, vec_ref, w1_ref, w2_ref, a1_ref,
                  a2_ref, g1_ref, g2_ref, bd_ref,
                  ro_ref, vo_ref, kko_ref, go_ref, wl0_ref, wl1_ref, kt0_ref, kt1_ref,
                  a0o_ref, a1o_ref):
    halo = halo_ref[...]
    mu = mu_ref[...]
    vec = vec_ref[...]

    def shifted(ref, j):
        u = ref[...]
        prev = _shift_rows(u, halo[0:1, j * BR:(j + 1) * BR], True)
        nxt = _shift_rows(u, halo[1:2, j * BR:(j + 1) * BR], False)
        return u, 0.5 * (prev + nxt) - u

    ur, xr = shifted(r_ref, 0)
    uk, xk = shifted(k_ref, 1)
    uv, xv = shifted(v_ref, 2)
    ux, xx = shifted(x_ref, 3)
    r = ur + xr * mu[0:1]
    k = uk + xk * mu[1:2]
    v = uv + xv * mu[2:3]
    xw = ux + xx * mu[3:4]
    xa = ux + xx * mu[4:5]
    xg = ux + xx * mu[5:6]
    kk = k * vec[0:1]
    ss = _dot_hi(kk * kk, bd_ref[...])
    kk = kk / jnp.maximum(jnp.sqrt(ss), 1e-12)
    ro_ref[...] = r
    vo_ref[...] = v
    kko_ref[...] = kk
    go_ref[...] = _dot(_sigmoid(_dot(xg, g1_ref[...])), g2_ref[...])
    w_raw = _dot(jnp.tanh(_dot(xw, w1_ref[...])), w2_ref[...])
    a_raw = _dot(_dot(xa, a1_ref[...]), a2_ref[...])
    k_a = vec[1:2]
    for d, (wl_ref, kt_ref, ao_ref) in enumerate(((wl0_ref, kt0_ref, a0o_ref),
                                                   (wl1_ref, kt1_ref, a1o_ref))):
        w_d = vec[2 + d:3 + d] + w_raw[:, d * BR:(d + 1) * BR]
        a_d = _sigmoid(vec[4 + d:5 + d] + a_raw[:, d * BR:(d + 1) * BR])
        wl_ref[...] = -jnp.exp(-_softplus(-w_d) - 0.5)
        kt_ref[...] = k * (1.0 + (a_d - 1.0) * k_a)
        ao_ref[...] = a_d


def _dot3(a, b):
    a_hi = a.astype(BF16)
    b_hi = b.astype(BF16)
    a_lo = (a - a_hi.astype(F32)).astype(BF16)
    b_lo = (b - b_hi.astype(F32)).astype(BF16)
    d = lambda p, q: jnp.dot(p, q, preferred_element_type=F32)
    return d(a_hi, b_hi) + (d(a_hi, b_lo) + d(a_lo, b_hi))


TRI_BASE = 2
RW_GROUP = 4
RW_GW = RW_GROUP * RW_HD


def _bdiag(x):
    n = x.shape[0]
    tiled = jnp.concatenate([x] * RW_GROUP, axis=0)
    ri = lax.broadcasted_iota(jnp.int32, tiled.shape, 0)
    ci = lax.broadcasted_iota(jnp.int32, tiled.shape, 1)
    blk = x.shape[1] // RW_GROUP
    same_block = (ri >> int(math.log2(n))) == (ci >> int(math.log2(blk)))
    return jnp.where(same_block, tiled, jnp.zeros_like(tiled))


def _mm_cat(a_cat, b_cat, dot=_dot):
    return dot(a_cat, _bdiag(b_cat))


def _tri_inverse_cat(l_cats, c):
    shape = l_cats[0].shape
    ri = lax.broadcasted_iota(jnp.int32, shape, 0)
    ci = lax.broadcasted_iota(jnp.int32, shape, 1) & (c - 1)
    same = lambda b: (ri >> int(math.log2(b))) == (ci >> int(math.log2(b)))
    eye = (ri == ci).astype(F32)
    ps = [jnp.where(same(TRI_BASE), l, 0.0) for l in l_cats]
    invs = [eye - p for p in ps]
    n = 2
    while n < TRI_BASE:
        ps = [_mm_cat(p, p, _dot3) for p in ps]
        invs = [inv + _mm_cat(inv, p, _dot3) for inv, p in zip(invs, ps)]
        n *= 2
    b = TRI_BASE
    while b < c:
        off_mask = jnp.logical_and(same(2 * b), jnp.logical_not(same(b)))
        t1 = [_mm_cat(inv, jnp.where(off_mask, l, 0.0)) for inv, l in zip(invs, l_cats)]
        t2 = [_mm_cat(t, inv) for t, inv in zip(t1, invs)]
        invs = [inv - t for inv, t in zip(invs, t2)]
        b *= 2
    return invs


def _rw_chunk_prepare(chunks):
    c = chunks[0][0].shape[0]
    assert c == RW_HD
    ri = lax.broadcasted_iota(jnp.int32, (c, RW_GW), 0)
    ci = lax.broadcasted_iota(jnp.int32, (c, RW_GW), 1) & (c - 1)
    n_grp = RW_HEADS // RW_GROUP
    items = []
    for r, wl, kt, v, kk, a, cum, rev in chunks:
        lc = _dot_hi(cum, wl)
        e_neg = jnp.exp(-lc)
        alpha = kk * jnp.exp(lc - wl)
        beta = a * kk * e_neg
        gamma = kt * e_neg
        rho = r * jnp.exp(lc)
        p_end = jnp.exp(lc[0:1] if rev else lc[c - 1:c])
        for grp in range(n_grp):
            sl = slice(grp * RW_GW, (grp + 1) * RW_GW)
            items.append(dict(al=alpha[:, sl], be=beta[:, sl], ga=gamma[:, sl], rh=rho[:, sl],
                              vv=v[:, sl], p_end=p_end[:, sl], rev=rev))
    gms = [_dot_nt(jnp.concatenate([it["al"], it["rh"]], axis=0),
                   jnp.concatenate([_bdiag(it["be"]), _bdiag(it["ga"])], axis=0)) for it in items]
    for it, gm in zip(items, gms):
        strict = (ci > ri) if it["rev"] else (ci < ri)
        incl = (ci >= ri) if it["rev"] else (ci <= ri)
        it["l_b"] = jnp.where(strict, gm[:c, :RW_GW], 0.0)
        it["l_g"] = jnp.where(strict, gm[:c, RW_GW:], 0.0)
        it["r_b"] = jnp.where(incl, gm[c:, :RW_GW], 0.0).astype(BF16)
        it["r_g"] = jnp.where(incl, gm[c:, RW_GW:], 0.0)
    lgv = [_mm_cat(it["l_g"], it["vv"]) for it in items]
    y0 = [_mm_cat(it["r_g"], it["vv"]) for it in items]
    upd0 = [_dot_tn(it["vv"], it["ga"]) for it in items]
    t_invs = _tri_inverse_cat([it["l_b"] for it in items], c)
    ws = [_mm_cat(t, it["al"]) for t, it in zip(t_invs, items)]
    u0 = [_mm_cat(t, x) for t, x in zip(t_invs, lgv)]
    out = [dict(w_rho=jnp.concatenate([w, it["rh"]], axis=0).astype(BF16), u0=u, y0=y,
                r_b=it["r_b"], be=it["be"].astype(BF16), upd0=up, p_end=it["p_end"])
           for it, w, u, y, up in zip(items, ws, u0, y0, upd0)]
    return [out[i * n_grp:(i + 1) * n_grp] for i in range(len(chunks))]


def _rw_chunk_apply(problems):
    bi = lax.broadcasted_iota(jnp.int32, (RW_GW, RW_GW), 0) >> int(math.log2(RW_HD))
    bj = lax.broadcasted_iota(jnp.int32, (RW_GW, RW_GW), 1) >> int(math.log2(RW_HD))
    diag_blocks = bi == bj
    items = [(g, st_ref, grp) for groups, st_ref in problems for grp, g in enumerate(groups)]
    sts = [st_ref[grp] for _, st_ref, grp in items]
    wss = [_dot_nt(g["w_rho"], st) for (g, _, _), st in zip(items, sts)]
    c = items[0][0]["u0"].shape[0]
    us = [ws[:c] + g["u0"] for (g, _, _), ws in zip(items, wss)]
    rbu = [_mm_cat(g["r_b"], u) for (g, _, _), u in zip(items, us)]
    utb = [_dot_tn(u, g["be"]) for (g, _, _), u in zip(items, us)]
    ys = [ws[c:] + g["y0"] - x for (g, _, _), ws, x in zip(items, wss, rbu)]
    for (g, st_ref, grp), st, x in zip(items, sts, utb):
        st_ref[grp] = (st + jnp.where(diag_blocks, g["upd0"] - x, 0.0)) * g["p_end"]
    n_grp = RW_HEADS // RW_GROUP
    return [jnp.concatenate(ys[i * n_grp:(i + 1) * n_grp], axis=-1) for i in range(len(problems))]


RW_SUB = 4


def _rw_scan_body(rf_ref, wlf_ref, ktf_ref, vf_ref, kkf_ref, af_ref,
                  rb_ref, wlb_ref, ktb_ref, vb_ref, kkb_ref, ab_ref, cum_ref,
                  yf_ref, yb_ref, sf_ref, sb_ref):
    @pl.when(pl.program_id(0) == 0)
    def _init():
        sf_ref[...] = jnp.zeros_like(sf_ref)
        sb_ref[...] = jnp.zeros_like(sb_ref)

    c = CHUNK
    fwd_refs = (rf_ref, wlf_ref, ktf_ref, vf_ref, kkf_ref, af_ref)
    bwd_refs = (rb_ref, wlb_ref, ktb_ref, vb_ref, kkb_ref, ab_ref)
    rows = lambda j: pl.ds(j * c, c)
    chunks = [tuple(ref[rows(j), :] for ref in fwd_refs) + (cum_ref[0], False) for j in range(RW_SUB)]
    chunks += [tuple(ref[rows(j), :] for ref in bwd_refs) + (cum_ref[1], True) for j in range(RW_SUB)]
    prep = _rw_chunk_prepare(chunks)
    for j in range(RW_SUB):
        jb = RW_SUB - 1 - j
        y_f, y_b = _rw_chunk_apply([(prep[j], sf_ref), (prep[RW_SUB + jb], sb_ref)])
        yf_ref[rows(j), :] = y_f
        yb_ref[rows(jb), :] = y_b


def _rw_post_body(yf_ref, yb_ref, r_ref, kt0_ref, kt1_ref, v_ref, g_ref, vec_ref, bd_ref, o_ref):
    y = yf_ref[...] + yb_ref[...]
    bd = bd_ref[...]
    vec = vec_ref[...]
    mean = _dot_hi(y, bd) * (1.0 / RW_HD)
    yc = y - mean
    var = _dot_hi(yc * yc, bd) * (1.0 / RW_HD)
    yn = yc * lax.rsqrt(var + RW_GN_EPS) * vec[0:1] + vec[1:2]
    bonus = _dot_hi(r_ref[...] * (kt0_ref[...] + kt1_ref[...]) * vec[2:3], bd) * v_ref[...]
    o_ref[...] = ((yn + bonus) * g_ref[...]).astype(o_ref.dtype)


def _rwkv7_mixer(cols, n_ctx, mu, w0, w1, w2, a0, a1, a2, g1, g2, k_k, k_a, r_k, ln_g, ln_b):
    S = cols.shape[0]
    tb = ROW_BLOCK
    nblk = S // tb
    halo = _halo_rows(cols, COL_B * BR, (COL_B + 4) * BR, n_ctx, tb, 1, 1)
    mu8 = jnp.zeros((8, BR), F32).at[:6].set(mu)
    vec = jnp.zeros((8, BR), F32).at[0].set(k_k).at[1].set(k_a).at[2:4].set(w0).at[4:6].set(a0)
    w1c = jnp.concatenate([w1[0], w1[1]], axis=1).astype(BF16)
    a1c = jnp.concatenate([a1[0], a1[1]], axis=1).astype(BF16)
    w2c = _block_diag(w2).astype(BF16)
    a2c = _block_diag(a2).astype(BF16)
    bd = jnp.kron(jnp.eye(RW_HEADS, dtype=F32), jnp.ones((RW_HD, RW_HD), F32))
    row = lambda col: pl.BlockSpec((tb, BR), lambda i: (i, col))
    full = lambda arr: pl.BlockSpec(arr.shape, lambda i: (0,) * arr.ndim)
    consts = [mu8, vec, w1c, w2c, a1c, a2c, g1.astype(BF16), g2.astype(BF16), bd]
    outs = pl.pallas_call(
        _rw_prep_body,
        grid=(nblk,),
        in_specs=[row(COL_B), row(COL_B + 1), row(COL_B + 2), row(COL_B + 3),
                  pl.BlockSpec((None, 8, 4 * BR), lambda i: (i, 0, 0))] + [full(x) for x in consts],
        out_specs=[row(0)] * 10,
        out_shape=[jax.ShapeDtypeStruct((S, BR), F32)] * 10,
        compiler_params=_cparams(1),
    )(cols, cols, cols, cols, halo, *consts)
    r, v, kk, g, wl0, wl1, kt0, kt1, av0, av1 = outs

    c = CHUNK
    rows = RW_SUB * c
    nch, ncc = S // rows, n_ctx // rows
    t = np.arange(c)
    cum = jnp.asarray(np.stack([(t[None, :] <= t[:, None]), (t[None, :] >= t[:, None])]).astype(np.float32))
    fwd = pl.BlockSpec((rows, BR), lambda i: (i, 0))
    bwd = pl.BlockSpec((rows, BR), lambda i: (_rev_block(i, ncc, nch), 0))
    y_f, y_b = pl.pallas_call(
        _rw_scan_body,
        grid=(nch,),
        in_specs=[fwd] * 6 + [bwd] * 6 + [pl.BlockSpec((2, c, c), lambda i: (0, 0, 0))],
        out_specs=[fwd, bwd],
        out_shape=[jax.ShapeDtypeStruct((S, BR), F32)] * 2,
        scratch_shapes=[pltpu.VMEM((RW_HEADS // RW_GROUP, RW_GW, RW_GW), F32)] * 2,
        compiler_params=_cparams(1),
    )(r, wl0, kt0, v, kk, av0, r, wl1, kt1, v, kk, av1, cum)

    pvec = jnp.zeros((8, BR), F32).at[0].set(ln_g).at[1].set(ln_b).at[2].set(r_k.reshape(-1))
    return pl.pallas_call(
        _rw_post_body,
        grid=(nblk,),
        in_specs=[row(0)] * 7 + [full(pvec), full(bd)],
        out_specs=row(0),
        out_shape=jax.ShapeDtypeStruct((S, BR), BF16),
        compiler_params=_cparams(1),
    )(y_f, y_b, r, kt0, kt1, v, g, pvec, bd)


def _merge_body(ya_ref, yb_ref, yc_ref, yd_ref, gate_ref, wb_ref, o_ref):
    d = o_ref.shape[1]
    acc = None
    for k, y_ref in enumerate((ya_ref, yb_ref, yc_ref, yd_ref)):
        p = jnp.dot(y_ref[...], wb_ref[k], preferred_element_type=F32)
        p = gate_ref[:, k * d:(k + 1) * d].astype(F32) * p
        acc = p if acc is None else acc + p
    o_ref[...] = acc.astype(o_ref.dtype)


def _merge(branches, gates, wb):
    S = gates.shape[0]
    d = wb.shape[2]
    tm = ROW_BLOCK
    row = pl.BlockSpec((tm, BR), lambda i: (i, 0))
    return pl.pallas_call(
        _merge_body,
        grid=(S // tm,),
        in_specs=[row] * 4 + [pl.BlockSpec((tm, N_BRANCH * d), lambda i: (i, 0)),
                              pl.BlockSpec(wb.shape, lambda i: (0, 0, 0))],
        out_specs=pl.BlockSpec((tm, d), lambda i: (i, 0)),
        out_shape=jax.ShapeDtypeStruct((S, d), BF16),
        compiler_params=_cparams(1),
    )(*branches, gates, wb)


def _expert_body(blk_e_ref, n_used_ref, slot_ref, nxt_ref, x_ref, wg_hbm, wu_hbm, wd_hbm, o_ref,
                 wgf_ref, wuf_ref, wdf_ref, wgb_ref, wub_ref, wdb_ref, sem_ref, *, layer):
    b = pl.program_id(0)
    e = blk_e_ref[b]
    new_expert = jnp.logical_or(b == 0, e != blk_e_ref[jnp.maximum(b - 1, 0)])
    used = b < n_used_ref[0]
    slot = slot_ref[b]

    def weight_copies(expert, s):
        return (pltpu.make_async_copy(wg_hbm.at[layer, expert], wgf_ref.at[s], sem_ref.at[s, 0]),
                pltpu.make_async_copy(wu_hbm.at[layer, expert], wuf_ref.at[s], sem_ref.at[s, 1]),
                pltpu.make_async_copy(wd_hbm.at[layer, expert], wdf_ref.at[s], sem_ref.at[s, 2]))

    @pl.when(jnp.logical_and(used, new_expert))
    def _switch_expert():
        @pl.when(b == 0)
        def _first_fetch():
            for cp in weight_copies(e, slot):
                cp.start()

        nxt = nxt_ref[b]

        @pl.when(nxt >= 0)
        def _prefetch_next():
            for cp in weight_copies(nxt, 1 - slot):
                cp.start()

        for cp in weight_copies(e, slot):
            cp.wait()
        wgb_ref[...] = wgf_ref[slot].astype(BF16)
        wub_ref[...] = wuf_ref[slot].astype(BF16)
        wdb_ref[...] = wdf_ref[slot].astype(BF16)

    @pl.when(used)
    def _compute():
        x = x_ref[...]
        hg = jnp.dot(x, wgb_ref[...], preferred_element_type=F32)
        hu = jnp.dot(x, wub_ref[...], preferred_element_type=F32)
        hidden = (_silu(hg) * hu).astype(BF16)
        o_ref[...] = jnp.dot(hidden, wdb_ref[...], preferred_element_type=F32).astype(o_ref.dtype)

    @pl.when(jnp.logical_not(used))
    def _unused():
        o_ref[...] = jnp.zeros_like(o_ref)


def _expert_ffn(buf, blk_e, n_used, w_gate, w_up, w_down, layer):
    n_rows, D = buf.shape
    n_blocks = n_rows // MOE_BLOCK
    de = w_gate.shape[-1]
    blk = jnp.arange(n_blocks, dtype=jnp.int32)
    change = jnp.concatenate([jnp.zeros((1,), jnp.int32), (blk_e[1:] != blk_e[:-1]).astype(jnp.int32)])
    slot = (jnp.cumsum(change) % 2).astype(jnp.int32)
    nxt_blk = jnp.sum(blk_e[None, :] <= blk_e[:, None], axis=1).astype(jnp.int32)
    nxt = jnp.where(nxt_blk < n_used[0], blk_e[jnp.minimum(nxt_blk, n_blocks - 1)], -1).astype(jnp.int32)
    grid_spec = pltpu.PrefetchScalarGridSpec(
        num_scalar_prefetch=4,
        grid=(n_blocks,),
        in_specs=[pl.BlockSpec((MOE_BLOCK, D), lambda b, *_: (b, 0)),
                  pl.BlockSpec(memory_space=pl.ANY),
                  pl.BlockSpec(memory_space=pl.ANY),
                  pl.BlockSpec(memory_space=pl.ANY)],
        out_specs=pl.BlockSpec((MOE_BLOCK, D), lambda b, *_: (b, 0)),
        scratch_shapes=[pltpu.VMEM((2, D, de), F32), pltpu.VMEM((2, D, de), F32), pltpu.VMEM((2, de, D), F32),
                        pltpu.VMEM((D, de), BF16), pltpu.VMEM((D, de), BF16), pltpu.VMEM((de, D), BF16),
                        pltpu.SemaphoreType.DMA((2, 3))],
    )
    return pl.pallas_call(
        functools.partial(_expert_body, layer=layer),
        grid_spec=grid_spec,
        out_shape=jax.ShapeDtypeStruct((n_rows, D), BF16),
        compiler_params=_cparams(1),
    )(blk_e, n_used, slot, nxt, buf, w_gate, w_up, w_down)


def _combine_body(x_ref, y0_ref, y1_ref, w_ref, mod_ref, o_ref, *, gate_idx):
    w = w_ref[...]
    y = y0_ref[...].astype(F32) * w[:, 0:1] + y1_ref[...].astype(F32) * w[:, 1:2]
    o_ref[...] = x_ref[...] + mod_ref[gate_idx:gate_idx + 1] * y


def _route_body(lg_ref, bias_ref, tri_ref, ints_ref, wts_ref, cnt_out_ref, cnt_ref):
    @pl.when(pl.program_id(0) == 0)
    def _init():
        cnt_ref[...] = jnp.zeros_like(cnt_ref)

    scores = _sigmoid(lg_ref[...])
    sel = scores + bias_ref[...]
    tb = scores.shape[1]
    epg = EXPERTS_PER_GROUP
    row = lax.broadcasted_iota(jnp.int32, (epg, tb), 0)
    best = e0 = e1 = w0 = w1 = None
    for g in range(N_GROUPS):
        s = sel[g * epg:(g + 1) * epg]
        sc = scores[g * epg:(g + 1) * epg]
        m1 = jnp.max(s, axis=0, keepdims=True)
        i1 = jnp.min(jnp.where(s == m1, row, epg), axis=0, keepdims=True)
        first = row == i1
        s2 = jnp.where(first, -jnp.inf, s)
        m2 = jnp.max(s2, axis=0, keepdims=True)
        i2 = jnp.min(jnp.where(s2 == m2, row, epg), axis=0, keepdims=True)
        second = row == i2
        gs = m1 + m2
        a0 = jnp.sum(jnp.where(first, sc, 0.0), axis=0, keepdims=True)
        a1 = jnp.sum(jnp.where(second, sc, 0.0), axis=0, keepdims=True)
        if g == 0:
            best, e0, e1, w0, w1 = gs, i1, i2, a0, a1
        else:
            take = gs > best
            best = jnp.where(take, gs, best)
            e0 = jnp.where(take, i1 + g * epg, e0)
            e1 = jnp.where(take, i2 + g * epg, e1)
            w0 = jnp.where(take, a0, w0)
            w1 = jnp.where(take, a1, w1)
    rows_e = lax.broadcasted_iota(jnp.int32, scores.shape, 0)
    oh0 = rows_e == e0
    oh1 = rows_e == e1
    oh = jnp.logical_or(oh0, oh1).astype(F32)
    before = jnp.dot(oh.astype(BF16), tri_ref[...], preferred_element_type=F32) + cnt_ref[:, 0:1]
    rank0 = jnp.sum(jnp.where(oh0, before, 0.0), axis=0, keepdims=True)
    rank1 = jnp.sum(jnp.where(oh1, before, 0.0), axis=0, keepdims=True)
    cnt_ref[...] = cnt_ref[...] + jnp.sum(oh, axis=1, keepdims=True)
    cnt_out_ref[...] = cnt_ref[...]
    wsum = w0 + w1
    zi = jnp.zeros((4, tb), jnp.int32)
    ints_ref[...] = jnp.concatenate([e0, e1, rank0.astype(jnp.int32), rank1.astype(jnp.int32), zi], axis=0)
    wts_ref[...] = jnp.concatenate([w0 / wsum, w1 / wsum, jnp.zeros((6, tb), F32)], axis=0)


def _route(logits_t, router_b):
    E, n = logits_t.shape
    tb = ROW_BLOCK
    t = np.arange(tb)
    tri = jnp.asarray((t[:, None] < t[None, :]).astype(np.float32), BF16)
    ints, wts, cnt = pl.pallas_call(
        _route_body,
        grid=(n // tb,),
        in_specs=[pl.BlockSpec((E, tb), lambda i: (0, i)),
                  pl.BlockSpec((E, 1), lambda i: (0, 0)),
                  pl.BlockSpec((tb, tb), lambda i: (0, 0))],
        out_specs=[pl.BlockSpec((8, tb), lambda i: (0, i)),
                   pl.BlockSpec((8, tb), lambda i: (0, i)),
                   pl.BlockSpec((E, 128), lambda i: (0, 0))],
        out_shape=[jax.ShapeDtypeStruct((8, n), jnp.int32), jax.ShapeDtypeStruct((8, n), F32),
                   jax.ShapeDtypeStruct((E, 128), F32)],
        scratch_shapes=[pltpu.VMEM((E, 128), F32)],
        compiler_params=_cparams(1),
    )(logits_t, router_b.astype(F32)[:, None], tri)
    return ints, wts, cnt[:, 0].astype(jnp.int32)


def _moe(xres, h, logits_t, router_b, w_gate, w_up, w_down, layer, mod, n_ctx_blocks, row0):
    S, D = xres.shape
    hh = h[row0:]
    n = hh.shape[0]
    ints, wts, sizes = _route(logits_t[:, row0:], router_b)
    n_assign = n * TOP_K
    padded = ((sizes + MOE_BLOCK - 1) // MOE_BLOCK) * MOE_BLOCK
    pad_ends = jnp.cumsum(padded)
    pad_starts = pad_ends - padded
    dest0 = pad_starts[ints[0]] + ints[2]
    dest1 = pad_starts[ints[1]] + ints[3]
    n_blocks = -(-n_assign // MOE_BLOCK) + N_EXPERTS
    tok = jnp.arange(n, dtype=jnp.int32)
    src = jnp.zeros((n_blocks * MOE_BLOCK,), jnp.int32).at[dest0].set(tok).at[dest1].set(tok)
    buf = jnp.concatenate([hh, hh, hh], axis=0)[src]
    blk_start = jnp.arange(n_blocks, dtype=jnp.int32) * MOE_BLOCK
    blk_e = jnp.minimum(jnp.sum(pad_ends[None, :] <= blk_start[:, None], axis=1),
                        N_EXPERTS - 1).astype(jnp.int32)
    n_used = (pad_ends[-1] // MOE_BLOCK).astype(jnp.int32).reshape(1)
    yb = _expert_ffn(buf, blk_e, n_used, w_gate, w_up, w_down, layer)
    yg = yb[jnp.concatenate([dest0, dest1])]
    tm = ROW_BLOCK
    r0 = row0 // tm
    row = pl.BlockSpec((tm, D), lambda i: (i, 0))
    row2 = pl.BlockSpec((tm, D), lambda i: (i + n // tm, 0))
    xrow = pl.BlockSpec((tm, D), lambda i: (i + r0, 0))
    out = pl.pallas_call(
        functools.partial(_combine_body, gate_idx=5),
        grid=(n // tm,),
        in_specs=[xrow, row, row2, pl.BlockSpec((tm, 8), lambda i: (i, 0)),
                  pl.BlockSpec((None, 8, D), lambda i: (_row_class(i + r0, n_ctx_blocks), 0, 0))],
        out_specs=row,
        out_shape=jax.ShapeDtypeStruct((n, D), F32),
        compiler_params=_cparams(1),
    )(xres, yg, yg, wts.T, mod)
    if row0:
        out = jnp.concatenate([xres[:row0], out], axis=0)
    return out


def _grid_sincos(rows, d_model):
    nf = d_model // 4
    omega = POS_BASE ** (-jnp.arange(nf, dtype=F32) / nf)
    ang_r = jnp.arange(rows, dtype=F32)[:, None] * omega
    ang_c = jnp.arange(GRID_W, dtype=F32)[:, None] * omega
    f_r = jnp.concatenate([jnp.sin(ang_r), jnp.cos(ang_r)], axis=-1)[:, None, :]
    f_c = jnp.concatenate([jnp.sin(ang_c), jnp.cos(ang_c)], axis=-1)[None, :, :]
    shape = (rows, GRID_W, 2 * nf)
    pos = jnp.concatenate([jnp.broadcast_to(f_r, shape), jnp.broadcast_to(f_c, shape)], axis=-1)
    return pos.reshape(rows * GRID_W, d_model)


def kernel(x, c, ctx, c_ctx, w_mod, b_mod, norm_mix_g, norm_ffn_g, w_in, w_branch, w_out, rw_mu, rw_w0, rw_w1, rw_w2, rw_a0, rw_a1, rw_a2, rw_g1, rw_g2, rw_kk, rw_ka, rw_rk, rw_ln_g, rw_ln_b, hg_lb_logits, hg_norm_g, lru_conv_w, lru_conv_b, lru_wa, lru_ba, lru_wx, lru_bx, lru_lam, router_w, router_b, moe_w_gate, moe_w_up, moe_w_down, final_norm_g):
    _, T, D = x.shape
    n_ctx = ctx.shape[1]
    depth = w_in.shape[0]
    S = n_ctx + T
    ncb = n_ctx // ROW_BLOCK
    n_mix = N_COL_BLOCKS * BR

    xs = jnp.concatenate([ctx[0], x[0] + _grid_sincos(T // GRID_W, D)], axis=0)
    lb_cum = jnp.cumsum(jax.nn.softmax(hg_lb_logits.astype(F32), axis=0), axis=0)
    hg_lb = lb_cum - lb_cum[:1]
    cvec = jnp.zeros((8, D), F32).at[0].set(jax.nn.silu(c_ctx)).at[1].set(jax.nn.silu(c[0]))

    for l in range(depth):
        last = l == depth - 1
        mod = _mm(cvec, w_mod, layer=l, n_out=N_MOD * D, tm=8, tn=1024, out_dtype=F32,
                  bias=b_mod[l][None])
        mod = jnp.zeros((2, 8, D), F32).at[:, :N_MOD].set(mod[:2].reshape(2, N_MOD, D))
        h = _norm_mod(xs, norm_mix_g[l][None], mod, shift_idx=0, scale_idx=1, out_dtype=BF16,
                      n_ctx_blocks=ncb)
        tm = _pick_tm(S)
        cols = _mm(h, w_in, layer=l, n_out=n_mix, tm=tm, tn=1024, out_dtype=F32)
        gates = _mm(h, w_in, layer=l, col0=n_mix // 1024, n_out=N_BRANCH * D, tm=tm, tn=1024,
                    out_dtype=BF16, act="sigmoid")
        y_a = _fnet_mixer(cols, n_ctx)
        y_b = _rwkv7_mixer(cols, n_ctx, rw_mu[l], rw_w0[l], rw_w1[l], rw_w2[l], rw_a0[l], rw_a1[l],
                           rw_a2[l], rw_g1[l], rw_g2[l], rw_kk[l], rw_ka[l], rw_rk[l], rw_ln_g[l],
                           rw_ln_b[l])
        y_c = _hgrn2_mixer(cols, n_ctx, hg_lb[l], hg_norm_g[l])
        y_d = _rglru_mixer(cols, n_ctx, lru_conv_w[l], lru_conv_b[l], lru_wa[l], lru_ba[l],
                           lru_wx[l], lru_bx[l], lru_lam[l])
        merged = _merge((y_a, y_b, y_c, y_d), gates, w_branch[l].astype(BF16))
        xs = _mm(merged, w_out, layer=l, n_out=D, tm=ROW_BLOCK, tn=1024, out_dtype=F32,
                 res=xs, scale=mod[:, 2:3], n_ctx_blocks=ncb)
        h2, logits = _norm_mod(xs, norm_ffn_g[l][None], mod, shift_idx=3, scale_idx=4,
                               out_dtype=BF16, n_ctx_blocks=ncb, router_w=router_w)
        xs = _moe(xs, h2, logits, router_b, moe_w_gate, moe_w_up, moe_w_down, l, mod, ncb,
                  n_ctx if last else 0)
    out = _norm_mod(xs[n_ctx:], final_norm_g[None], jnp.zeros((2, 8, D), F32), shift_idx=None,
                    scale_idx=None, out_dtype=F32, n_ctx_blocks=0)
    return out[None]
```

```python
import functools
import math

import numpy as np
import jax
import jax.numpy as jnp
from jax import lax
from jax.experimental import pallas as pl
from jax.experimental.pallas import tpu as pltpu

F32 = jnp.float32
BF16 = jnp.bfloat16
HIGHEST = lax.Precision.HIGHEST

NORM_EPS = 1e-6
GRID_W = 64
POS_BASE = 10000.0
N_MOD = 6
BR = 512
N_BRANCH = 4
FN_GROUPS = 4
FN_GC = BR // FN_GROUPS
RW_HEADS = 8
RW_HD = BR // RW_HEADS
RW_GN_EPS = 64e-5
HG_HEADS = 4
HG_DK = BR // HG_HEADS
LRU_BLOCKS = 8
LRU_C = 8.0
COL_A, COL_B, COL_C, COL_D = 0, 1, 5, 10
N_COL_BLOCKS = 12
N_EXPERTS = 64
N_GROUPS = 8
EXPERTS_PER_GROUP = N_EXPERTS // N_GROUPS
TOP_K = 2
MOE_BLOCK = 128

ROW_BLOCK = 256
CHUNK = 64
FFT_N2 = 64
VMEM_LIMIT = 56 * 1024 * 1024
NEG_BIG = -1e30


def _cparams(n_axes):
    return pltpu.CompilerParams(dimension_semantics=("arbitrary",) * n_axes,
                                vmem_limit_bytes=VMEM_LIMIT)


def _rev_block(g, n_ctx_blocks, n_blocks):
    return jnp.where(g < n_ctx_blocks, n_ctx_blocks - 1 - g, n_blocks - 1 - (g - n_ctx_blocks))


def _row_class(i, n_ctx_blocks):
    return jnp.where(i < n_ctx_blocks, 0, 1)


def _sigmoid(x):
    return jax.nn.sigmoid(x)


def _silu(x):
    return x * jax.nn.sigmoid(x)


def _log_sigmoid(x):
    return jnp.minimum(x, 0.0) - jnp.log1p(jnp.exp(-jnp.abs(x)))


def _softplus(x):
    return jnp.maximum(x, 0.0) + jnp.log1p(jnp.exp(-jnp.abs(x)))


def _dot(a, b):
    return jnp.dot(a.astype(BF16), b.astype(BF16), preferred_element_type=F32)


def _dot_nt(a, b):
    return lax.dot_general(a.astype(BF16), b.astype(BF16), (((1,), (1,)), ((), ())),
                           preferred_element_type=F32)


def _dot_tn(a, b):
    return lax.dot_general(a.astype(BF16), b.astype(BF16), (((0,), (0,)), ((), ())),
                           preferred_element_type=F32)


def _dot_hi(a, b):
    return jnp.dot(a, b, precision=HIGHEST, preferred_element_type=F32)


def _mm_body(*refs, act, has_bias, has_res):
    a_ref, b_ref = refs[0], refs[1]
    k = 2
    bias_ref = res_ref = scale_ref = None
    if has_bias:
        bias_ref = refs[k]
        k += 1
    if has_res:
        res_ref, scale_ref = refs[k], refs[k + 1]
        k += 2
    o_ref, wb_ref = refs[k], refs[k + 1]

    @pl.when(pl.program_id(1) == 0)
    def _cast_weights():
        wb_ref[...] = b_ref[...].astype(BF16)

    acc = jnp.dot(a_ref[...].astype(BF16), wb_ref[...], preferred_element_type=F32)
    if has_bias:
        acc = acc + bias_ref[...]
    if act == "sigmoid":
        acc = _sigmoid(acc)
    if has_res:
        acc = res_ref[...] + scale_ref[...] * acc
    o_ref[...] = acc.astype(o_ref.dtype)


def _pick_tm(m):
    return next(t for t in (768, 512, 256, 128, 64) if m % t == 0)


def _mm(a, b, *, n_out, tm, tn, out_dtype, layer=None, col0=0, act=None, bias=None,
        res=None, scale=None, n_ctx_blocks=1):
    M, K = a.shape
    assert M % tm == 0 and n_out % tn == 0
    grid = (n_out // tn, M // tm)
    in_specs = [pl.BlockSpec((tm, K), lambda j, i: (i, 0))]
    if layer is None:
        in_specs.append(pl.BlockSpec((K, tn), lambda j, i: (0, j + col0)))
    else:
        in_specs.append(pl.BlockSpec((None, K, tn), lambda j, i: (layer, 0, j + col0)))
    args = [a, b]
    if bias is not None:
        in_specs.append(pl.BlockSpec((1, tn), lambda j, i: (0, j)))
        args.append(bias)
    if res is not None:
        in_specs.append(pl.BlockSpec((tm, tn), lambda j, i: (i, j)))
        in_specs.append(pl.BlockSpec((None, 1, tn), lambda j, i: (_row_class(i, n_ctx_blocks), 0, j)))
        args += [res, scale]
    return pl.pallas_call(
        functools.partial(_mm_body, act=act, has_bias=bias is not None, has_res=res is not None),
        grid=grid,
        in_specs=in_specs,
        out_specs=pl.BlockSpec((tm, tn), lambda j, i: (i, j)),
        out_shape=jax.ShapeDtypeStruct((M, n_out), out_dtype),
        scratch_shapes=[pltpu.VMEM((K, tn), BF16)],
        compiler_params=_cparams(2),
    )(*args)


def _norm_body(*refs, shift_idx, scale_idx, with_router):
    if with_router:
        x_ref, g_ref, mod_ref, rw_ref, o_ref, lg_ref = refs
    else:
        x_ref, g_ref, mod_ref, o_ref = refs
    x = x_ref[...]
    y = x * lax.rsqrt(jnp.mean(x * x, axis=-1, keepdims=True) + NORM_EPS) * g_ref[...]
    if scale_idx is not None:
        m = mod_ref[...]
        y = y * (1.0 + m[scale_idx:scale_idx + 1]) + m[shift_idx:shift_idx + 1]
    o_ref[...] = y.astype(o_ref.dtype)
    if with_router:
        lg_ref[...] = lax.dot_general(rw_ref[...], y, (((1,), (1,)), ((), ())), precision=HIGHEST,
                                      preferred_element_type=F32)


def _norm_mod(x, g, mod, *, shift_idx, scale_idx, out_dtype, n_ctx_blocks, router_w=None):
    S, D = x.shape
    tm = ROW_BLOCK
    in_specs = [pl.BlockSpec((tm, D), lambda i: (i, 0)),
                pl.BlockSpec((1, D), lambda i: (0, 0)),
                pl.BlockSpec((None, 8, D), lambda i: (_row_class(i, n_ctx_blocks), 0, 0))]
    args = [x, g, mod]
    out_specs = pl.BlockSpec((tm, D), lambda i: (i, 0))
    out_shape = jax.ShapeDtypeStruct((S, D), out_dtype)
    if router_w is not None:
        in_specs.append(pl.BlockSpec((N_EXPERTS, D), lambda i: (0, 0)))
        args.append(router_w.T)
        out_specs = [out_specs, pl.BlockSpec((N_EXPERTS, tm), lambda i: (0, i))]
        out_shape = [out_shape, jax.ShapeDtypeStruct((N_EXPERTS, S), F32)]
    return pl.pallas_call(
        functools.partial(_norm_body, shift_idx=shift_idx, scale_idx=scale_idx,
                          with_router=router_w is not None),
        grid=(S // tm,),
        in_specs=in_specs,
        out_specs=out_specs,
        out_shape=out_shape,
        compiler_params=_cparams(1),
    )(*args)


def _dft_tables(n, scale):
    k = jnp.arange(n, dtype=jnp.int32)
    ang = ((k[:, None] * k[None, :]) % n).astype(F32) * (2.0 * math.pi / n)
    return jnp.cos(ang) * scale, jnp.sin(ang) * scale


def _channel_dft_tables():
    c, s = _dft_tables(FN_GC, FN_GC ** -0.5)
    eye = jnp.eye(FN_GROUPS, dtype=F32)
    return jnp.kron(eye, c), jnp.kron(eye, s)


def _fnet_dense_body(u_ref, cc_ref, sc_ref, cl_ref, sl_ref, o_ref):
    u = u_ref[...]
    ac = _dot_hi(u, cc_ref[...])
    as_ = _dot_hi(u, sc_ref[...])
    o_ref[...] = _dot_hi(cl_ref[...], ac) - _dot_hi(sl_ref[...], as_)


def _fnet_dense(u, cc, sc):
    L = u.shape[0]
    cl, sl = _dft_tables(L, L ** -0.5)
    return pl.pallas_call(
        _fnet_dense_body,
        out_shape=jax.ShapeDtypeStruct((L, BR), F32),
        compiler_params=pltpu.CompilerParams(vmem_limit_bytes=VMEM_LIMIT),
    )(u, cc, sc, cl, sl)


def _fnet_stage1_body(u_ref, gr_ref, gi_ref, cc_ref, sc_ref, yr_ref, yi_ref):
    u = u_ref[...]
    zr = _dot3(u, cc_ref[...])
    zi = -_dot3(u, sc_ref[...])
    gr, gi = gr_ref[...], gi_ref[...]
    yr_ref[...] = _dot3(gr, zr) - _dot3(gi, zi)
    yi_ref[...] = _dot3(gr, zi) + _dot3(gi, zr)


def _fnet_stage2_body(yr_ref, yi_ref, c2_ref, s2_ref, o_ref):
    c2, s2 = c2_ref[...], s2_ref[...]
    for j in range(8):
        o_ref[:, j, :] = _dot3(c2, yr_ref[j]) + _dot3(s2, yi_ref[j])


def _fnet_two_stage(u, cc, sc):
    L = u.shape[0]
    n2 = FFT_N2
    n1 = L // n2
    assert n1 * n2 == L and n1 % 8 == 0
    f1 = jnp.arange(n1, dtype=jnp.int32)
    t1 = jnp.arange(n1, dtype=jnp.int32)
    t2 = jnp.arange(n2, dtype=jnp.int32)
    t = n2 * t1[None, None, :] + t2[:, None, None]
    ang = ((f1[None, :, None] * t) % L).astype(F32) * (2.0 * math.pi / L)
    gr = jnp.cos(ang) * (L ** -0.5)
    gi = -jnp.sin(ang) * (L ** -0.5)
    c2, s2 = _dft_tables(n2, 1.0)
    u2 = u.reshape(n1, n2 * BR)
    yr, yi = pl.pallas_call(
        _fnet_stage1_body,
        grid=(n2,),
        in_specs=[pl.BlockSpec((n1, BR), lambda j: (0, j)),
                  pl.BlockSpec((None, n1, n1), lambda j: (j, 0, 0)),
                  pl.BlockSpec((None, n1, n1), lambda j: (j, 0, 0)),
                  pl.BlockSpec((BR, BR), lambda j: (0, 0)),
                  pl.BlockSpec((BR, BR), lambda j: (0, 0))],
        out_specs=[pl.BlockSpec((n1, BR), lambda j: (0, j)),
                   pl.BlockSpec((n1, BR), lambda j: (0, j))],
        out_shape=[jax.ShapeDtypeStruct((n1, n2 * BR), F32)] * 2,
        compiler_params=_cparams(1),
    )(u2, gr, gi, cc, sc)
    out = pl.pallas_call(
        _fnet_stage2_body,
        grid=(n1 // 8,),
        in_specs=[pl.BlockSpec((8, n2, BR), lambda i: (i, 0, 0)),
                  pl.BlockSpec((8, n2, BR), lambda i: (i, 0, 0)),
                  pl.BlockSpec((n2, n2), lambda i: (0, 0)),
                  pl.BlockSpec((n2, n2), lambda i: (0, 0))],
        out_specs=pl.BlockSpec((n2, 8, BR), lambda i: (0, i, 0)),
        out_shape=jax.ShapeDtypeStruct((n2, n1, BR), F32),
        compiler_params=_cparams(1),
    )(yr.reshape(n1, n2, BR), yi.reshape(n1, n2, BR), c2, s2)
    return out.reshape(L, BR)


def _fnet_mixer(cols, n_ctx):
    cc, sc = _channel_dft_tables()
    ua = cols[:, COL_A * BR:(COL_A + 1) * BR]
    y_ctx = _fnet_dense(ua[:n_ctx], cc, sc)
    y_lat = _fnet_two_stage(ua[n_ctx:], cc, sc)
    return jnp.concatenate([y_ctx, y_lat], axis=0).astype(BF16)


def _halo_rows(x, c0, c1, n_ctx, tb, before, after):
    C = c1 - c0
    blocks = x.reshape(-1, tb, x.shape[1])
    ncb = n_ctx // tb
    parts = []
    for blk in (blocks[:ncb], blocks[ncb:]):
        prev = jnp.concatenate([jnp.zeros((1, before, C), x.dtype), blk[:-1, tb - before:, c0:c1]], axis=0)
        nxt = jnp.concatenate([blk[1:, :after, c0:c1], jnp.zeros((1, after, C), x.dtype)], axis=0)
        pad = jnp.zeros((blk.shape[0], 8 - before - after, C), x.dtype)
        parts.append(jnp.concatenate([prev, nxt, pad], axis=1))
    return jnp.concatenate(parts, axis=0)


def _shift_rows(x, halo_row, down):
    n = x.shape[0]
    row = lax.broadcasted_iota(jnp.int32, (n, 1), 0)
    if down:
        return jnp.where(row == 0, halo_row, pltpu.roll(x, 1, axis=0))
    return jnp.where(row == n - 1, halo_row, pltpu.roll(x, n - 1, axis=0))


def _lru_prep_body(x_ref, halo_ref, cw_ref, cb_ref, w4_ref, b4_ref, lam_ref,
                   af_ref, bf_ref, ab_ref, bb_ref):
    x = x_ref[...]
    halo = halo_ref[...]
    x_m1 = _shift_rows(x, halo[1:2], True)
    x_m2 = _shift_rows(x_m1, halo[0:1], True)
    x_p1 = _shift_rows(x, halo[2:3], False)
    cw = cw_ref[...]
    xc = cb_ref[...] + x_m2 * cw[0:1] + x_m1 * cw[1:2] + x * cw[2:3] + x_p1 * cw[3:4]
    pre = _sigmoid(_dot(xc, w4_ref[...]) + b4_ref[...])
    sp = _softplus(-lam_ref[...])
    for d, (a_ref, b_ref) in enumerate(((af_ref, bf_ref), (ab_ref, bb_ref))):
        r = pre[:, d * BR:(d + 1) * BR]
        i = pre[:, (2 + d) * BR:(3 + d) * BR]
        log_a = (-LRU_C) * r * sp[d:d + 1]
        a_ref[...] = jnp.exp(log_a)
        b_ref[...] = jnp.sqrt(jnp.maximum(1.0 - jnp.exp(2.0 * log_a), 0.0)) * i * xc


def _lru_scan_body(af_ref, bf_ref, ab_ref, bb_ref, hf_ref, hb_ref, sf_ref, sb_ref):
    @pl.when(pl.program_id(0) == 0)
    def _init():
        sf_ref[...] = jnp.zeros_like(sf_ref)
        sb_ref[...] = jnp.zeros_like(sb_ref)

    tb = af_ref.shape[0]

    def step(t, carry):
        hf, hb = carry
        hf = af_ref[pl.ds(t, 1), :] * hf + bf_ref[pl.ds(t, 1), :]
        hf_ref[pl.ds(t, 1), :] = hf
        u = tb - 1 - t
        hb = ab_ref[pl.ds(u, 1), :] * hb + bb_ref[pl.ds(u, 1), :]
        hb_ref[pl.ds(u, 1), :] = hb
        return hf, hb

    hf, hb = lax.fori_loop(0, tb, step, (sf_ref[...], sb_ref[...]), unroll=8)
    sf_ref[...] = hf
    sb_ref[...] = hb


def _gelu_tanh(x):
    return 0.5 * x * (1.0 + jnp.tanh(math.sqrt(2.0 / math.pi) * (x + 0.044715 * (x * x * x))))


def _lru_post_body(hf_ref, hb_ref, g_ref, o_ref):
    o_ref[...] = ((hf_ref[...] + hb_ref[...]) * _gelu_tanh(g_ref[...])).astype(o_ref.dtype)


def _block_diag(w):
    G, n, m = w.shape
    eye = jnp.eye(G, dtype=w.dtype)
    return (eye[:, None, :, None] * w[:, :, None, :]).reshape(G * n, G * m)


def _rglru_mixer(cols, n_ctx, conv_w, conv_b, wa, ba, wx, bx, lam):
    S = cols.shape[0]
    tb = ROW_BLOCK
    nblk, ncb = S // tb, n_ctx // tb
    halo = _halo_rows(cols, COL_D * BR, (COL_D + 1) * BR, n_ctx, tb, 2, 1)
    w4 = jnp.concatenate([_block_diag(wa[0]), _block_diag(wa[1]),
                          _block_diag(wx[0]), _block_diag(wx[1])], axis=1).astype(BF16)
    b4 = jnp.concatenate([ba[0], ba[1], bx[0], bx[1]])[None]
    cw8 = jnp.zeros((8, BR), F32).at[:4].set(conv_w)
    lam8 = jnp.zeros((8, BR), F32).at[:2].set(lam)
    row = lambda c: pl.BlockSpec((tb, BR), lambda i: (i, c))
    full = lambda shp: pl.BlockSpec(shp, lambda i: (0,) * len(shp))
    a_f, b_f, a_b, b_b = pl.pallas_call(
        _lru_prep_body,
        grid=(nblk,),
        in_specs=[row(COL_D), pl.BlockSpec((None, 8, BR), lambda i: (i, 0, 0)),
                  full((8, BR)), full((1, BR)), full((BR, 4 * BR)), full((1, 4 * BR)), full((8, BR))],
        out_specs=[row(0)] * 4,
        out_shape=[jax.ShapeDtypeStruct((S, BR), F32)] * 4,
        compiler_params=_cparams(1),
    )(cols, halo, cw8, conv_b[None], w4, b4, lam8)
    fwd = pl.BlockSpec((tb, BR), lambda g: (g, 0))
    bwd = pl.BlockSpec((tb, BR), lambda g: (_rev_block(g, ncb, nblk), 0))
    h_f, h_b = pl.pallas_call(
        _lru_scan_body,
        grid=(nblk,),
        in_specs=[fwd, fwd, bwd, bwd],
        out_specs=[fwd, bwd],
        out_shape=[jax.ShapeDtypeStruct((S, BR), F32)] * 2,
        scratch_shapes=[pltpu.VMEM((1, BR), F32), pltpu.VMEM((1, BR), F32)],
        compiler_params=_cparams(1),
    )(a_f, b_f, a_b, b_b)
    return pl.pallas_call(
        _lru_post_body,
        grid=(nblk,),
        in_specs=[row(0), row(0), row(COL_D + 1)],
        out_specs=row(0),
        out_shape=jax.ShapeDtypeStruct((S, BR), BF16),
        compiler_params=_cparams(1),
    )(h_f, h_b, cols)


def _level_sizes(c):
    out, b = [], c // 2
    while b >= 1:
        out.append(b)
        b //= 2
    return out


def _hgrn_tables(c):
    t = np.arange(c)
    sizes = _level_sizes(c)
    mats = np.zeros((2, (1 + len(sizes)) * c, c), np.float32)
    masks = np.zeros((len(sizes), c, c), np.float32)
    cum = [(t[None, :] <= t[:, None]).astype(np.float32), (t[None, :] >= t[:, None]).astype(np.float32)]
    for d in range(2):
        mats[d, :c] = cum[d]
        for l, b in enumerate(sizes):
            ref = (t // (2 * b)) * (2 * b) + (b - 1 if d == 0 else b)
            mats[d, (l + 1) * c:(l + 2) * c] = cum[d] - cum[d][ref]
    for l, b in enumerate(sizes):
        masks[l] = (t[:, None] // (2 * b) == t[None, :] // (2 * b)).astype(np.float32)
    return jnp.asarray(mats), jnp.asarray(masks)


def _hgrn_prepare(chunks, lb_ref, masks_ref):
    c = chunks[0][0].shape[0]
    sizes = _level_sizes(c)
    log_lb, log_1mlb, one_m_lb = lb_ref[0:1], lb_ref[1:2], lb_ref[2:3]
    row = lax.broadcasted_iota(jnp.int32, (c, 1), 0)
    eye = (lax.broadcasted_iota(jnp.int32, (c, c), 0) == lax.broadcasted_iota(jnp.int32, (c, c), 1))
    items = []
    for q_raw, z, v, mats, rev in chunks:
        q = _silu(q_raw)
        a = log_lb
        b = log_1mlb + _log_sigmoid(z)
        lf = jnp.maximum(a, b) + jnp.log1p(jnp.exp(-jnp.abs(a - b)))
        k = one_m_lb * _sigmoid(-z)
        dall = _dot_hi(mats, lf)
        g = dall[:c]
        g_last = g[0:1] if rev else g[c - 1:c]
        qg = (q * jnp.exp(g)).astype(BF16)
        kg = (k * jnp.exp(g_last - g)).astype(BF16)
        dec = jnp.exp(g_last)
        q_lv, k_lv = [], []
        for l, b in enumerate(sizes):
            dl = dall[(l + 1) * c:(l + 2) * c]
            second = ((row >> int(math.log2(b))) & 1) == 1
            q_side = jnp.logical_not(second) if rev else second
            k_side = jnp.logical_not(q_side)
            q_lv.append((q * jnp.exp(jnp.where(q_side, dl, NEG_BIG))).astype(BF16))
            k_lv.append((k * jnp.exp(jnp.where(k_side, -dl, NEG_BIG))).astype(BF16))
        qk = q * k
        vb = v.astype(BF16)
        for h in range(HG_HEADS):
            sl = slice(h * HG_DK, (h + 1) * HG_DK)
            items.append(dict(qg=qg[:, sl], kg=kg[:, sl], dec=dec[:, sl], v=vb[:, sl],
                              diag=jnp.sum(qk[:, sl], axis=-1, keepdims=True),
                              q_lv=[x[:, sl] for x in q_lv], k_lv=[x[:, sl] for x in k_lv]))
    prods = [[_dot_nt(it["q_lv"][l], it["k_lv"][l]) for it in items] for l in range(len(sizes))]
    scores = []
    for i, it in enumerate(items):
        s = jnp.where(eye, it["diag"], 0.0)
        for l in range(len(sizes)):
            s = s + masks_ref[l] * prods[l][i]
        scores.append(s)
    y_intra = [_dot(s, it["v"]) for s, it in zip(scores, items)]
    kv = [_dot_tn(it["v"], it["kg"]) for it in items]
    out = [dict(qg=it["qg"], dec=it["dec"], y_intra=y, kv=x) for it, y, x in zip(items, y_intra, kv)]
    return [out[i * HG_HEADS:(i + 1) * HG_HEADS] for i in range(len(chunks))]


def _hgrn_apply(problems):
    items = [(g, st_ref, h) for heads, st_ref in problems for h, g in enumerate(heads)]
    sts = [st_ref[h] for _, st_ref, h in items]
    ys = [_dot_nt(g["qg"], st) + g["y_intra"] for (g, _, _), st in zip(items, sts)]
    for (g, st_ref, h), st in zip(items, sts):
        st_ref[h] = st * g["dec"] + g["kv"]
    return [jnp.concatenate(ys[i * HG_HEADS:(i + 1) * HG_HEADS], axis=-1) for i in range(len(problems))]


HG_SUB = 4


def _hgrn_scan_body(qf_ref, zf_ref, vf_ref, qb_ref, zb_ref, vb_ref, lb_ref, mats_ref, masks_ref,
                    yf_ref, yb_ref, sf_ref, sb_ref):
    @pl.when(pl.program_id(0) == 0)
    def _init():
        sf_ref[...] = jnp.zeros_like(sf_ref)
        sb_ref[...] = jnp.zeros_like(sb_ref)

    c = CHUNK
    rows = lambda j: pl.ds(j * c, c)
    chunks = [(qf_ref[rows(j), :], zf_ref[rows(j), :], vf_ref[rows(j), :], mats_ref[0], False)
              for j in range(HG_SUB)]
    chunks += [(qb_ref[rows(j), :], zb_ref[rows(j), :], vb_ref[rows(j), :], mats_ref[1], True)
               for j in range(HG_SUB)]
    prep = _hgrn_prepare(chunks, lb_ref, masks_ref)
    for j in range(HG_SUB):
        jb = HG_SUB - 1 - j
        y_f, y_b = _hgrn_apply([(prep[j], sf_ref), (prep[HG_SUB + jb], sb_ref)])
        yf_ref[rows(j), :] = y_f
        yb_ref[rows(jb), :] = y_b


def _hgrn_post_body(yf_ref, yb_ref, og_ref, ng_ref, o_ref):
    y = yf_ref[...] + yb_ref[...]
    outs = []
    for h in range(HG_HEADS):
        yh = y[:, h * HG_DK:(h + 1) * HG_DK]
        outs.append(yh * lax.rsqrt(jnp.mean(yh * yh, axis=-1, keepdims=True) + NORM_EPS) * ng_ref[...])
    o_ref[...] = (jnp.concatenate(outs, axis=-1) * _silu(og_ref[...])).astype(o_ref.dtype)


def _hgrn2_mixer(cols, n_ctx, lb, norm_g):
    S = cols.shape[0]
    rows = HG_SUB * CHUNK
    nch, ncc = S // rows, n_ctx // rows
    mats, masks = _hgrn_tables(CHUNK)
    lb8 = jnp.zeros((8, BR), F32).at[0].set(jnp.log(lb)).at[1].set(jnp.log1p(-lb)).at[2].set(1.0 - lb)
    fwd = lambda col: pl.BlockSpec((rows, BR), lambda g: (g, col))
    bwd = lambda col: pl.BlockSpec((rows, BR), lambda g: (_rev_block(g, ncc, nch), col))
    full = lambda shp: pl.BlockSpec(shp, lambda g: (0,) * len(shp))
    y_f, y_b = pl.pallas_call(
        _hgrn_scan_body,
        grid=(nch,),
        in_specs=[fwd(COL_C), fwd(COL_C + 1), fwd(COL_C + 3),
                  bwd(COL_C), bwd(COL_C + 2), bwd(COL_C + 3),
                  full((8, BR)), full(mats.shape), full(masks.shape)],
        out_specs=[fwd(0), bwd(0)],
        out_shape=[jax.ShapeDtypeStruct((S, BR), F32)] * 2,
        scratch_shapes=[pltpu.VMEM((HG_HEADS, HG_DK, HG_DK), F32)] * 2,
        compiler_params=_cparams(1),
    )(cols, cols, cols, cols, cols, cols, lb8, mats, masks)
    tb = ROW_BLOCK
    row = lambda col: pl.BlockSpec((tb, BR), lambda i: (i, col))
    return pl.pallas_call(
        _hgrn_post_body,
        grid=(S // tb,),
        in_specs=[row(0), row(0), row(COL_C + 4), pl.BlockSpec((1, HG_DK), lambda i: (0, 0))],
        out_specs=row(0),
        out_shape=jax.ShapeDtypeStruct((S, BR), BF16),
        compiler_params=_cparams(1),
    )(y_f, y_b, cols, norm_g[None])


def _rw_prep_body(r_ref, k_ref, v_ref, x_ref, halo_ref, mu_ref, vec_ref, w1_ref, w2_ref, a1_ref,
                  a2_ref, g1_ref, g2_ref, bd_ref,
                  ro_ref, vo_ref, kko_ref, go_ref, wl0_ref, wl1_ref, kt0_ref, kt1_ref,
                  a0o_ref, a1o_ref):
    halo = halo_ref[...]
    mu = mu_ref[...]
    vec = vec_ref[...]

    def shifted(ref, j):
        u = ref[...]
        prev = _shift_rows(u, halo[0:1, j * BR:(j + 1) * BR], True)
        nxt = _shift_rows(u, halo[1:2, j * BR:(j + 1) * BR], False)
        return u, 0.5 * (prev + nxt) - u

    ur, xr = shifted(r_ref, 0)
    uk, xk = shifted(k_ref, 1)
    uv, xv = shifted(v_ref, 2)
    ux, xx = shifted(x_ref, 3)
    r = ur + xr * mu[0:1]
    k = uk + xk * mu[1:2]
    v = uv + xv * mu[2:3]
    xw = ux + xx * mu[3:4]
    xa = ux + xx * mu[4:5]
    xg = ux + xx * mu[5:6]
    kk = k * vec[0:1]
    ss = _dot3(kk * kk, bd_ref[...])
    kk = kk / jnp.maximum(jnp.sqrt(ss), 1e-12)
    ro_ref[...] = r
    vo_ref[...] = v
    kko_ref[...] = kk
    go_ref[...] = _dot(_sigmoid(_dot(xg, g1_ref[...])), g2_ref[...])
    w_raw = _dot(jnp.tanh(_dot(xw, w1_ref[...])), w2_ref[...])
    a_raw = _dot(_dot(xa, a1_ref[...]), a2_ref[...])
    k_a = vec[1:2]
    for d, (wl_ref, kt_ref, ao_ref) in enumerate(((wl0_ref, kt0_ref, a0o_ref),
                                                   (wl1_ref, kt1_ref, a1o_ref))):
        w_d = vec[2 + d:3 + d] + w_raw[:, d * BR:(d + 1) * BR]
        a_d = _sigmoid(vec[4 + d:5 + d] + a_raw[:, d * BR:(d + 1) * BR])
        wl_ref[...] = -jnp.exp(-_softplus(-w_d) - 0.5)
        kt_ref[...] = k * (1.0 + (a_d - 1.0) * k_a)
        ao_ref[...] = a_d


def _dot3(a, b):
    a_hi = a.astype(BF16)
    b_hi = b.astype(BF16)
    a_lo = (a - a_hi.astype(F32)).astype(BF16)
    b_lo = (b - b_hi.astype(F32)).astype(BF16)
    d = lambda p, q: jnp.dot(p, q, preferred_element_type=F32)
    return d(a_hi, b_hi) + (d(a_hi, b_lo) + d(a_lo, b_hi))


TRI_BASE = 2
RW_GROUP = 4
RW_GW = RW_GROUP * RW_HD


def _bdiag(x):
    n = x.shape[0]
    tiled = jnp.concatenate([x] * RW_GROUP, axis=0)
    ri = lax.broadcasted_iota(jnp.int32, tiled.shape, 0)
    ci = lax.broadcasted_iota(jnp.int32, tiled.shape, 1)
    blk = x.shape[1] // RW_GROUP
    same_block = (ri >> int(math.log2(n))) == (ci >> int(math.log2(blk)))
    return jnp.where(same_block, tiled, jnp.zeros_like(tiled))


def _mm_cat(a_cat, b_cat, dot=_dot):
    return dot(a_cat, _bdiag(b_cat))


def _tri_inverse_cat(l_cats, c):
    shape = l_cats[0].shape
    ri = lax.broadcasted_iota(jnp.int32, shape, 0)
    ci = lax.broadcasted_iota(jnp.int32, shape, 1) & (c - 1)
    same = lambda b: (ri >> int(math.log2(b))) == (ci >> int(math.log2(b)))
    eye = (ri == ci).astype(F32)
    ps = [jnp.where(same(TRI_BASE), l, 0.0) for l in l_cats]
    invs = [eye - p for p in ps]
    n = 2
    while n < TRI_BASE:
        ps = [_mm_cat(p, p, _dot3) for p in ps]
        invs = [inv + _mm_cat(inv, p, _dot3) for inv, p in zip(invs, ps)]
        n *= 2
    b = TRI_BASE
    while b < c:
        off_mask = jnp.logical_and(same(2 * b), jnp.logical_not(same(b)))
        t1 = [_mm_cat(inv, jnp.where(off_mask, l, 0.0)) for inv, l in zip(invs, l_cats)]
        t2 = [_mm_cat(t, inv) for t, inv in zip(t1, invs)]
        invs = [inv - t for inv, t in zip(invs, t2)]
        b *= 2
    return invs


def _rw_chunk_prepare(chunks):
    c = chunks[0][0].shape[0]
    assert c == RW_HD
    ri = lax.broadcasted_iota(jnp.int32, (c, RW_GW), 0)
    ci = lax.broadcasted_iota(jnp.int32, (c, RW_GW), 1) & (c - 1)
    n_grp = RW_HEADS // RW_GROUP
    items = []
    for r, wl, kt, v, kk, a, cum, rev in chunks:
        lc = _dot_hi(cum, wl)
        e_neg = jnp.exp(-lc)
        alpha = kk * jnp.exp(lc - wl)
        beta = a * kk * e_neg
        gamma = kt * e_neg
        rho = r * jnp.exp(lc)
        p_end = jnp.exp(lc[0:1] if rev else lc[c - 1:c])
        for grp in range(n_grp):
            sl = slice(grp * RW_GW, (grp + 1) * RW_GW)
            items.append(dict(al=alpha[:, sl], be=beta[:, sl], ga=gamma[:, sl], rh=rho[:, sl],
                              vv=v[:, sl], p_end=p_end[:, sl], rev=rev))
    gms = [_dot_nt(jnp.concatenate([it["al"], it["rh"]], axis=0),
                   jnp.concatenate([_bdiag(it["be"]), _bdiag(it["ga"])], axis=0)) for it in items]
    for it, gm in zip(items, gms):
        strict = (ci > ri) if it["rev"] else (ci < ri)
        incl = (ci >= ri) if it["rev"] else (ci <= ri)
        it["l_b"] = jnp.where(strict, gm[:c, :RW_GW], 0.0)
        it["l_g"] = jnp.where(strict, gm[:c, RW_GW:], 0.0)
        it["r_b"] = jnp.where(incl, gm[c:, :RW_GW], 0.0).astype(BF16)
        it["r_g"] = jnp.where(incl, gm[c:, RW_GW:], 0.0)
    lgv = [_mm_cat(it["l_g"], it["vv"]) for it in items]
    y0 = [_mm_cat(it["r_g"], it["vv"]) for it in items]
    upd0 = [_dot_tn(it["vv"], it["ga"]) for it in items]
    t_invs = _tri_inverse_cat([it["l_b"] for it in items], c)
    ws = [_mm_cat(t, it["al"]) for t, it in zip(t_invs, items)]
    u0 = [_mm_cat(t, x) for t, x in zip(t_invs, lgv)]
    out = [dict(w_rho=jnp.concatenate([w, it["rh"]], axis=0).astype(BF16), u0=u, y0=y,
                r_b=it["r_b"], be=it["be"].astype(BF16), upd0=up, p_end=it["p_end"])
           for it, w, u, y, up in zip(items, ws, u0, y0, upd0)]
    return [out[i * n_grp:(i + 1) * n_grp] for i in range(len(chunks))]


def _rw_chunk_apply(problems):
    bi = lax.broadcasted_iota(jnp.int32, (RW_GW, RW_GW), 0) >> int(math.log2(RW_HD))
    bj = lax.broadcasted_iota(jnp.int32, (RW_GW, RW_GW), 1) >> int(math.log2(RW_HD))
    diag_blocks = bi == bj
    items = [(g, st_ref, grp) for groups, st_ref in problems for grp, g in enumerate(groups)]
    sts = [st_ref[grp] for _, st_ref, grp in items]
    wss = [_dot_nt(g["w_rho"], st) for (g, _, _), st in zip(items, sts)]
    c = items[0][0]["u0"].shape[0]
    us = [ws[:c] + g["u0"] for (g, _, _), ws in zip(items, wss)]
    rbu = [_mm_cat(g["r_b"], u) for (g, _, _), u in zip(items, us)]
    utb = [_dot_tn(u, g["be"]) for (g, _, _), u in zip(items, us)]
    ys = [ws[c:] + g["y0"] - x for (g, _, _), ws, x in zip(items, wss, rbu)]
    for (g, st_ref, grp), st, x in zip(items, sts, utb):
        st_ref[grp] = (st + jnp.where(diag_blocks, g["upd0"] - x, 0.0)) * g["p_end"]
    n_grp = RW_HEADS // RW_GROUP
    return [jnp.concatenate(ys[i * n_grp:(i + 1) * n_grp], axis=-1) for i in range(len(problems))]


RW_SUB = 4


def _rw_scan_body(rf_ref, wlf_ref, ktf_ref, vf_ref, kkf_ref, af_ref,
                  rb_ref, wlb_ref, ktb_ref, vb_ref, kkb_ref, ab_ref, cum_ref,
                  yf_ref, yb_ref, sf_ref, sb_ref):
    @pl.when(pl.program_id(0) == 0)
    def _init():
        sf_ref[...] = jnp.zeros_like(sf_ref)
        sb_ref[...] = jnp.zeros_like(sb_ref)

    c = CHUNK
    fwd_refs = (rf_ref, wlf_ref, ktf_ref, vf_ref, kkf_ref, af_ref)
    bwd_refs = (rb_ref, wlb_ref, ktb_ref, vb_ref, kkb_ref, ab_ref)
    rows = lambda j: pl.ds(j * c, c)
    chunks = [tuple(ref[rows(j), :] for ref in fwd_refs) + (cum_ref[0], False) for j in range(RW_SUB)]
    chunks += [tuple(ref[rows(j), :] for ref in bwd_refs) + (cum_ref[1], True) for j in range(RW_SUB)]
    prep = _rw_chunk_prepare(chunks)
    for j in range(RW_SUB):
        jb = RW_SUB - 1 - j
        y_f, y_b = _rw_chunk_apply([(prep[j], sf_ref), (prep[RW_SUB + jb], sb_ref)])
        yf_ref[rows(j), :] = y_f
        yb_ref[rows(jb), :] = y_b


def _rw_post_body(yf_ref, yb_ref, r_ref, kt0_ref, kt1_ref, v_ref, g_ref, vec_ref, bd_ref, o_ref):
    y = yf_ref[...] + yb_ref[...]
    bd = bd_ref[...]
    vec = vec_ref[...]
    mean = _dot3(y, bd) * (1.0 / RW_HD)
    yc = y - mean
    var = _dot3(yc * yc, bd) * (1.0 / RW_HD)
    yn = yc * lax.rsqrt(var + RW_GN_EPS) * vec[0:1] + vec[1:2]
    bonus = _dot3(r_ref[...] * (kt0_ref[...] + kt1_ref[...]) * vec[2:3], bd) * v_ref[...]
    o_ref[...] = ((yn + bonus) * g_ref[...]).astype(o_ref.dtype)


def _rwkv7_mixer(cols, n_ctx, mu, w0, w1, w2, a0, a1, a2, g1, g2, k_k, k_a, r_k, ln_g, ln_b):
    S = cols.shape[0]
    tb = ROW_BLOCK
    nblk = S // tb
    halo = _halo_rows(cols, COL_B * BR, (COL_B + 4) * BR, n_ctx, tb, 1, 1)
    mu8 = jnp.zeros((8, BR), F32).at[:6].set(mu)
    vec = jnp.zeros((8, BR), F32).at[0].set(k_k).at[1].set(k_a).at[2:4].set(w0).at[4:6].set(a0)
    w1c = jnp.concatenate([w1[0], w1[1]], axis=1).astype(BF16)
    a1c = jnp.concatenate([a1[0], a1[1]], axis=1).astype(BF16)
    w2c = _block_diag(w2).astype(BF16)
    a2c = _block_diag(a2).astype(BF16)
    bd = jnp.kron(jnp.eye(RW_HEADS, dtype=F32), jnp.ones((RW_HD, RW_HD), F32))
    row = lambda col: pl.BlockSpec((tb, BR), lambda i: (i, col))
    full = lambda arr: pl.BlockSpec(arr.shape, lambda i: (0,) * arr.ndim)
    consts = [mu8, vec, w1c, w2c, a1c, a2c, g1.astype(BF16), g2.astype(BF16), bd]
    outs = pl.pallas_call(
        _rw_prep_body,
        grid=(nblk,),
        in_specs=[row(COL_B), row(COL_B + 1), row(COL_B + 2), row(COL_B + 3),
                  pl.BlockSpec((None, 8, 4 * BR), lambda i: (i, 0, 0))] + [full(x) for x in consts],
        out_specs=[row(0)] * 10,
        out_shape=[jax.ShapeDtypeStruct((S, BR), F32)] * 10,
        compiler_params=_cparams(1),
    )(cols, cols, cols, cols, halo, *consts)
    r, v, kk, g, wl0, wl1, kt0, kt1, av0, av1 = outs

    c = CHUNK
    rows = RW_SUB * c
    nch, ncc = S // rows, n_ctx // rows
    t = np.arange(c)
    cum = jnp.asarray(np.stack([(t[None, :] <= t[:, None]), (t[None, :] >= t[:, None])]).astype(np.float32))
    fwd = pl.BlockSpec((rows, BR), lambda i: (i, 0))
    bwd = pl.BlockSpec((rows, BR), lambda i: (_rev_block(i, ncc, nch), 0))
    y_f, y_b = pl.pallas_call(
        _rw_scan_body,
        grid=(nch,),
        in_specs=[fwd] * 6 + [bwd] * 6 + [pl.BlockSpec((2, c, c), lambda i: (0, 0, 0))],
        out_specs=[fwd, bwd],
        out_shape=[jax.ShapeDtypeStruct((S, BR), F32)] * 2,
        scratch_shapes=[pltpu.VMEM((RW_HEADS // RW_GROUP, RW_GW, RW_GW), F32)] * 2,
        compiler_params=_cparams(1),
    )(r, wl0, kt0, v, kk, av0, r, wl1, kt1, v, kk, av1, cum)

    pvec = jnp.zeros((8, BR), F32).at[0].set(ln_g).at[1].set(ln_b).at[2].set(r_k.reshape(-1))
    return pl.pallas_call(
        _rw_post_body,
        grid=(nblk,),
        in_specs=[row(0)] * 7 + [full(pvec), full(bd)],
        out_specs=row(0),
        out_shape=jax.ShapeDtypeStruct((S, BR), BF16),
        compiler_params=_cparams(1),
    )(y_f, y_b, r, kt0, kt1, v, g, pvec, bd)


def _merge_body(ya_ref, yb_ref, yc_ref, yd_ref, gate_ref, wb_ref, o_ref):
    d = o_ref.shape[1]
    acc = None
    for k, y_ref in enumerate((ya_ref, yb_ref, yc_ref, yd_ref)):
        p = jnp.dot(y_ref[...], wb_ref[k], preferred_element_type=F32)
        p = gate_ref[:, k * d:(k + 1) * d].astype(F32) * p
        acc = p if acc is None else acc + p
    o_ref[...] = acc.astype(o_ref.dtype)


def _merge(branches, gates, wb):
    S = gates.shape[0]
    d = wb.shape[2]
    tm = ROW_BLOCK
    row = pl.BlockSpec((tm, BR), lambda i: (i, 0))
    return pl.pallas_call(
        _merge_body,
        grid=(S // tm,),
        in_specs=[row] * 4 + [pl.BlockSpec((tm, N_BRANCH * d), lambda i: (i, 0)),
                              pl.BlockSpec(wb.shape, lambda i: (0, 0, 0))],
        out_specs=pl.BlockSpec((tm, d), lambda i: (i, 0)),
        out_shape=jax.ShapeDtypeStruct((S, d), BF16),
        compiler_params=_cparams(1),
    )(*branches, gates, wb)


def _expert_body(blk_e_ref, n_used_ref, slot_ref, nxt_ref, src_ref, h_hbm, wg_hbm, wu_hbm, wd_hbm, o_ref,
                 xbuf_ref, wgf_ref, wuf_ref, wdf_ref, wgb_ref, wub_ref, wdb_ref, sem_ref, xsem_ref, *,
                 layer):
    b = pl.program_id(0)
    e = blk_e_ref[b]
    new_expert = jnp.logical_or(b == 0, e != blk_e_ref[jnp.maximum(b - 1, 0)])
    n_used = n_used_ref[0]
    used = b < n_used
    slot = slot_ref[b]
    xslot = b % 2

    def row_copy(blk, i, xs):
        r = src_ref[blk * MOE_BLOCK + i]
        return pltpu.make_async_copy(h_hbm.at[pl.ds(r, 1)], xbuf_ref.at[xs, pl.ds(i, 1)], xsem_ref.at[xs])

    def start_rows(blk, xs):
        for i in range(MOE_BLOCK):
            row_copy(blk, i, xs).start()

    @pl.when(b == 0)
    def _first_rows():
        start_rows(0, 0)

    @pl.when(b + 1 < n_used)
    def _next_rows():
        start_rows(b + 1, 1 - xslot)

    def weight_copies(expert, s):
        return (pltpu.make_async_copy(wg_hbm.at[layer, expert], wgf_ref.at[s], sem_ref.at[s, 0]),
                pltpu.make_async_copy(wu_hbm.at[layer, expert], wuf_ref.at[s], sem_ref.at[s, 1]),
                pltpu.make_async_copy(wd_hbm.at[layer, expert], wdf_ref.at[s], sem_ref.at[s, 2]))

    @pl.when(jnp.logical_and(used, new_expert))
    def _switch_expert():
        @pl.when(b == 0)
        def _first_fetch():
            for cp in weight_copies(e, slot):
                cp.start()

        nxt = nxt_ref[b]

        @pl.when(nxt >= 0)
        def _prefetch_next():
            for cp in weight_copies(nxt, 1 - slot):
                cp.start()

        for cp in weight_copies(e, slot):
            cp.wait()
        wgb_ref[...] = wgf_ref[slot].astype(BF16)
        wub_ref[...] = wuf_ref[slot].astype(BF16)
        wdb_ref[...] = wdf_ref[slot].astype(BF16)

    @pl.when(used)
    def _compute():
        pltpu.make_async_copy(h_hbm.at[pl.ds(0, MOE_BLOCK)], xbuf_ref.at[xslot], xsem_ref.at[xslot]).wait()
        x = xbuf_ref[xslot].astype(BF16)
        hg = jnp.dot(x, wgb_ref[...], preferred_element_type=F32)
        hu = jnp.dot(x, wub_ref[...], preferred_element_type=F32)
        hidden = (_silu(hg) * hu).astype(BF16)
        o_ref[...] = jnp.dot(hidden, wdb_ref[...], preferred_element_type=F32).astype(o_ref.dtype)

    @pl.when(jnp.logical_not(used))
    def _unused():
        o_ref[...] = jnp.zeros_like(o_ref)


def _expert_ffn(h, src, blk_e, n_used, w_gate, w_up, w_down, layer):
    D = h.shape[1]
    n_rows = src.shape[0]
    n_blocks = n_rows // MOE_BLOCK
    de = w_gate.shape[-1]
    blk = jnp.arange(n_blocks, dtype=jnp.int32)
    change = jnp.concatenate([jnp.zeros((1,), jnp.int32), (blk_e[1:] != blk_e[:-1]).astype(jnp.int32)])
    slot = (jnp.cumsum(change) % 2).astype(jnp.int32)
    nxt_blk = jnp.sum(blk_e[None, :] <= blk_e[:, None], axis=1).astype(jnp.int32)
    nxt = jnp.where(nxt_blk < n_used[0], blk_e[jnp.minimum(nxt_blk, n_blocks - 1)], -1).astype(jnp.int32)
    grid_spec = pltpu.PrefetchScalarGridSpec(
        num_scalar_prefetch=5,
        grid=(n_blocks,),
        in_specs=[pl.BlockSpec(memory_space=pl.ANY),
                  pl.BlockSpec(memory_space=pl.ANY),
                  pl.BlockSpec(memory_space=pl.ANY),
                  pl.BlockSpec(memory_space=pl.ANY)],
        out_specs=pl.BlockSpec((MOE_BLOCK, D), lambda b, *_: (b, 0)),
        scratch_shapes=[pltpu.VMEM((2, MOE_BLOCK, D), F32),
                        pltpu.VMEM((2, D, de), F32), pltpu.VMEM((2, D, de), F32), pltpu.VMEM((2, de, D), F32),
                        pltpu.VMEM((D, de), BF16), pltpu.VMEM((D, de), BF16), pltpu.VMEM((de, D), BF16),
                        pltpu.SemaphoreType.DMA((2, 3)), pltpu.SemaphoreType.DMA((2,))],
    )
    return pl.pallas_call(
        functools.partial(_expert_body, layer=layer),
        grid_spec=grid_spec,
        out_shape=jax.ShapeDtypeStruct((n_rows, D), BF16),
        compiler_params=_cparams(1),
    )(blk_e, n_used, slot, nxt, src, h, w_gate, w_up, w_down)


def _combine_body(x_ref, y0_ref, y1_ref, w_ref, mod_ref, o_ref, *, gate_idx):
    w = w_ref[...]
    y = y0_ref[...].astype(F32) * w[:, 0:1] + y1_ref[...].astype(F32) * w[:, 1:2]
    o_ref[...] = x_ref[...] + mod_ref[gate_idx:gate_idx + 1] * y


def _route_body(lg_ref, bias_ref, tri_ref, ints_ref, wts_ref, cnt_out_ref, cnt_ref):
    @pl.when(pl.program_id(0) == 0)
    def _init():
        cnt_ref[...] = jnp.zeros_like(cnt_ref)

    scores = _sigmoid(lg_ref[...])
    sel = scores + bias_ref[...]
    tb = scores.shape[1]
    epg = EXPERTS_PER_GROUP
    row = lax.broadcasted_iota(jnp.int32, (epg, tb), 0)
    best = e0 = e1 = w0 = w1 = None
    for g in range(N_GROUPS):
        s = sel[g * epg:(g + 1) * epg]
        sc = scores[g * epg:(g + 1) * epg]
        m1 = jnp.max(s, axis=0, keepdims=True)
        i1 = jnp.min(jnp.where(s == m1, row, epg), axis=0, keepdims=True)
        first = row == i1
        s2 = jnp.where(first, -jnp.inf, s)
        m2 = jnp.max(s2, axis=0, keepdims=True)
        i2 = jnp.min(jnp.where(s2 == m2, row, epg), axis=0, keepdims=True)
        second = row == i2
        gs = m1 + m2
        a0 = jnp.sum(jnp.where(first, sc, 0.0), axis=0, keepdims=True)
        a1 = jnp.sum(jnp.where(second, sc, 0.0), axis=0, keepdims=True)
        if g == 0:
            best, e0, e1, w0, w1 = gs, i1, i2, a0, a1
        else:
            take = gs > best
            best = jnp.where(take, gs, best)
            e0 = jnp.where(take, i1 + g * epg, e0)
            e1 = jnp.where(take, i2 + g * epg, e1)
            w0 = jnp.where(take, a0, w0)
            w1 = jnp.where(take, a1, w1)
    rows_e = lax.broadcasted_iota(jnp.int32, scores.shape, 0)
    oh0 = rows_e == e0
    oh1 = rows_e == e1
    oh = jnp.logical_or(oh0, oh1).astype(F32)
    before = jnp.dot(oh.astype(BF16), tri_ref[...], preferred_element_type=F32) + cnt_ref[:, 0:1]
    rank0 = jnp.sum(jnp.where(oh0, before, 0.0), axis=0, keepdims=True)
    rank1 = jnp.sum(jnp.where(oh1, before, 0.0), axis=0, keepdims=True)
    cnt_ref[...] = cnt_ref[...] + jnp.sum(oh, axis=1, keepdims=True)
    cnt_out_ref[...] = cnt_ref[...]
    wsum = w0 + w1
    zi = jnp.zeros((4, tb), jnp.int32)
    ints_ref[...] = jnp.concatenate([e0, e1, rank0.astype(jnp.int32), rank1.astype(jnp.int32), zi], axis=0)
    wts_ref[...] = jnp.concatenate([w0 / wsum, w1 / wsum, jnp.zeros((6, tb), F32)], axis=0)


def _route(logits_t, router_b):
    E, n = logits_t.shape
    tb = ROW_BLOCK
    t = np.arange(tb)
    tri = jnp.asarray((t[:, None] < t[None, :]).astype(np.float32), BF16)
    ints, wts, cnt = pl.pallas_call(
        _route_body,
        grid=(n // tb,),
        in_specs=[pl.BlockSpec((E, tb), lambda i: (0, i)),
                  pl.BlockSpec((E, 1), lambda i: (0, 0)),
                  pl.BlockSpec((tb, tb), lambda i: (0, 0))],
        out_specs=[pl.BlockSpec((8, tb), lambda i: (0, i)),
                   pl.BlockSpec((8, tb), lambda i: (0, i)),
                   pl.BlockSpec((E, 128), lambda i: (0, 0))],
        out_shape=[jax.ShapeDtypeStruct((8, n), jnp.int32), jax.ShapeDtypeStruct((8, n), F32),
                   jax.ShapeDtypeStruct((E, 128), F32)],
        scratch_shapes=[pltpu.VMEM((E, 128), F32)],
        compiler_params=_cparams(1),
    )(logits_t, router_b.astype(F32)[:, None], tri)
    return ints, wts, cnt[:, 0].astype(jnp.int32)


def _moe(xres, h, logits_t, router_b, w_gate, w_up, w_down, layer, mod, n_ctx_blocks, row0):
    S, D = xres.shape
    n = S - row0
    ints, wts, sizes = _route(logits_t[:, row0:], router_b)
    n_assign = n * TOP_K
    padded = ((sizes + MOE_BLOCK - 1) // MOE_BLOCK) * MOE_BLOCK
    pad_ends = jnp.cumsum(padded)
    pad_starts = pad_ends - padded
    dest0 = pad_starts[ints[0]] + ints[2]
    dest1 = pad_starts[ints[1]] + ints[3]
    n_blocks = -(-n_assign // MOE_BLOCK) + N_EXPERTS
    tok = jnp.arange(n, dtype=jnp.int32) + row0
    src = jnp.full((n_blocks * MOE_BLOCK,), row0, jnp.int32).at[dest0].set(tok).at[dest1].set(tok)
    blk_start = jnp.arange(n_blocks, dtype=jnp.int32) * MOE_BLOCK
    blk_e = jnp.minimum(jnp.sum(pad_ends[None, :] <= blk_start[:, None], axis=1),
                        N_EXPERTS - 1).astype(jnp.int32)
    n_used = (pad_ends[-1] // MOE_BLOCK).astype(jnp.int32).reshape(1)
    yb = _expert_ffn(h, src, blk_e, n_used, w_gate, w_up, w_down, layer)
    yg = yb[jnp.concatenate([dest0, dest1])]
    tm = ROW_BLOCK
    r0 = row0 // tm
    row = pl.BlockSpec((tm, D), lambda i: (i, 0))
    row2 = pl.BlockSpec((tm, D), lambda i: (i + n // tm, 0))
    xrow = pl.BlockSpec((tm, D), lambda i: (i + r0, 0))
    out = pl.pallas_call(
        functools.partial(_combine_body, gate_idx=5),
        grid=(n // tm,),
        in_specs=[xrow, row, row2, pl.BlockSpec((tm, 8), lambda i: (i, 0)),
                  pl.BlockSpec((None, 8, D), lambda i: (_row_class(i + r0, n_ctx_blocks), 0, 0))],
        out_specs=row,
        out_shape=jax.ShapeDtypeStruct((n, D), F32),
        compiler_params=_cparams(1),
    )(xres, yg, yg, wts.T, mod)
    if row0:
        out = jnp.concatenate([xres[:row0], out], axis=0)
    return out


def _grid_sincos(rows, d_model):
    nf = d_model // 4
    omega = POS_BASE ** (-jnp.arange(nf, dtype=F32) / nf)
    ang_r = jnp.arange(rows, dtype=F32)[:, None] * omega
    ang_c = jnp.arange(GRID_W, dtype=F32)[:, None] * omega
    f_r = jnp.concatenate([jnp.sin(ang_r), jnp.cos(ang_r)], axis=-1)[:, None, :]
    f_c = jnp.concatenate([jnp.sin(ang_c), jnp.cos(ang_c)], axis=-1)[None, :, :]
    shape = (rows, GRID_W, 2 * nf)
    pos = jnp.concatenate([jnp.broadcast_to(f_r, shape), jnp.broadcast_to(f_c, shape)], axis=-1)
    return pos.reshape(rows * GRID_W, d_model)


def kernel(x, c, ctx, c_ctx, w_mod, b_mod, norm_mix_g, norm_ffn_g, w_in, w_branch, w_out, rw_mu, rw_w0, rw_w1, rw_w2, rw_a0, rw_a1, rw_a2, rw_g1, rw_g2, rw_kk, rw_ka, rw_rk, rw_ln_g, rw_ln_b, hg_lb_logits, hg_norm_g, lru_conv_w, lru_conv_b, lru_wa, lru_ba, lru_wx, lru_bx, lru_lam, router_w, router_b, moe_w_gate, moe_w_up, moe_w_down, final_norm_g):
    _, T, D = x.shape
    n_ctx = ctx.shape[1]
    depth = w_in.shape[0]
    S = n_ctx + T
    ncb = n_ctx // ROW_BLOCK
    n_mix = N_COL_BLOCKS * BR

    xs = jnp.concatenate([ctx[0], x[0] + _grid_sincos(T // GRID_W, D)], axis=0)
    lb_cum = jnp.cumsum(jax.nn.softmax(hg_lb_logits.astype(F32), axis=0), axis=0)
    hg_lb = lb_cum - lb_cum[:1]
    cvec = jnp.zeros((8, D), F32).at[0].set(jax.nn.silu(c_ctx)).at[1].set(jax.nn.silu(c[0]))

    for l in range(depth):
        last = l == depth - 1
        mod = _mm(cvec, w_mod, layer=l, n_out=N_MOD * D, tm=8, tn=1024, out_dtype=F32,
                  bias=b_mod[l][None])
        mod = jnp.zeros((2, 8, D), F32).at[:, :N_MOD].set(mod[:2].reshape(2, N_MOD, D))
        h = _norm_mod(xs, norm_mix_g[l][None], mod, shift_idx=0, scale_idx=1, out_dtype=BF16,
                      n_ctx_blocks=ncb)
        tm = _pick_tm(S)
        cols = _mm(h, w_in, layer=l, n_out=n_mix, tm=tm, tn=1024, out_dtype=F32)
        gates = _mm(h, w_in, layer=l, col0=n_mix // 1024, n_out=N_BRANCH * D, tm=tm, tn=1024,
                    out_dtype=BF16, act="sigmoid")
        y_a = _fnet_mixer(cols, n_ctx)
        y_b = _rwkv7_mixer(cols, n_ctx, rw_mu[l], rw_w0[l], rw_w1[l], rw_w2[l], rw_a0[l], rw_a1[l],
                           rw_a2[l], rw_g1[l], rw_g2[l], rw_kk[l], rw_ka[l], rw_rk[l], rw_ln_g[l],
                           rw_ln_b[l])
        y_c = _hgrn2_mixer(cols, n_ctx, hg_lb[l], hg_norm_g[l])
        y_d = _rglru_mixer(cols, n_ctx, lru_conv_w[l], lru_conv_b[l], lru_wa[l], lru_ba[l],
                           lru_wx[l], lru_bx[l], lru_lam[l])
        merged = _merge((y_a, y_b, y_c, y_d), gates, w_branch[l].astype(BF16))
        xs = _mm(merged, w_out, layer=l, n_out=D, tm=ROW_BLOCK, tn=1024, out_dtype=F32,
                 res=xs, scale=mod[:, 2:3], n_ctx_blocks=ncb)
        h2, logits = _norm_mod(xs, norm_ffn_g[l][None], mod, shift_idx=3, scale_idx=4,
                               out_dtype=F32, n_ctx_blocks=ncb, router_w=router_w)
        xs = _moe(xs, h2, logits, router_b, moe_w_gate, moe_w_up, moe_w_down, l, mod, ncb,
                  n_ctx if last else 0)
    out = _norm_mod(xs[n_ctx:], final_norm_g[None], jnp.zeros((2, 8, D), F32), shift_idx=None,
                    scale_idx=None, out_dtype=F32, n_ctx_blocks=0)
    return out[None]
```

```python
import functools
import math

import numpy as np
import jax
import jax.numpy as jnp
from jax import lax
from jax.experimental import pallas as pl
from jax.experimental.pallas import tpu as pltpu

F32 = jnp.float32
BF16 = jnp.bfloat16
HIGHEST = lax.Precision.HIGHEST

NORM_EPS = 1e-6
GRID_W = 64
POS_BASE = 10000.0
N_MOD = 6
BR = 512
N_BRANCH = 4
FN_GROUPS = 4
FN_GC = BR // FN_GROUPS
RW_HEADS = 8
RW_HD = BR // RW_HEADS
RW_GN_EPS = 64e-5
HG_HEADS = 4
HG_DK = BR // HG_HEADS
LRU_BLOCKS = 8
LRU_C = 8.0
COL_A, COL_B, COL_C, COL_D = 0, 1, 5, 10
N_COL_BLOCKS = 12
N_EXPERTS = 64
N_GROUPS = 8
EXPERTS_PER_GROUP = N_EXPERTS // N_GROUPS
TOP_K = 2
MOE_BLOCK = 128

ROW_BLOCK = 256
CHUNK = 64
FFT_N2 = 64
VMEM_LIMIT = 56 * 1024 * 1024
NEG_BIG = -1e30


def _cparams(n_axes):
    return pltpu.CompilerParams(dimension_semantics=("arbitrary",) * n_axes,
                                vmem_limit_bytes=VMEM_LIMIT)


def _rev_block(g, n_ctx_blocks, n_blocks):
    return jnp.where(g < n_ctx_blocks, n_ctx_blocks - 1 - g, n_blocks - 1 - (g - n_ctx_blocks))


def _row_class(i, n_ctx_blocks):
    return jnp.where(i < n_ctx_blocks, 0, 1)


def _sigmoid(x):
    return jax.nn.sigmoid(x)


def _silu(x):
    return x * jax.nn.sigmoid(x)


def _log_sigmoid(x):
    return jnp.minimum(x, 0.0) - jnp.log1p(jnp.exp(-jnp.abs(x)))


def _softplus(x):
    return jnp.maximum(x, 0.0) + jnp.log1p(jnp.exp(-jnp.abs(x)))


def _dot(a, b):
    return jnp.dot(a.astype(BF16), b.astype(BF16), preferred_element_type=F32)


def _dot_nt(a, b):
    return lax.dot_general(a.astype(BF16), b.astype(BF16), (((1,), (1,)), ((), ())),
                           preferred_element_type=F32)


def _dot_tn(a, b):
    return lax.dot_general(a.astype(BF16), b.astype(BF16), (((0,), (0,)), ((), ())),
                           preferred_element_type=F32)


def _dot_hi(a, b):
    return jnp.dot(a, b, precision=HIGHEST, preferred_element_type=F32)


def _mm_body(*refs, act, has_bias, has_res):
    a_ref, b_ref = refs[0], refs[1]
    k = 2
    bias_ref = res_ref = scale_ref = None
    if has_bias:
        bias_ref = refs[k]
        k += 1
    if has_res:
        res_ref, scale_ref = refs[k], refs[k + 1]
        k += 2
    o_ref, wb_ref = refs[k], refs[k + 1]

    @pl.when(pl.program_id(1) == 0)
    def _cast_weights():
        wb_ref[...] = b_ref[...].astype(BF16)

    acc = jnp.dot(a_ref[...].astype(BF16), wb_ref[...], preferred_element_type=F32)
    if has_bias:
        acc = acc + bias_ref[...]
    if act == "sigmoid":
        acc = _sigmoid(acc)
    if has_res:
        acc = res_ref[...] + scale_ref[...] * acc
    o_ref[...] = acc.astype(o_ref.dtype)


def _pick_tm(m):
    return next(t for t in (1408, 768, 512, 256, 128, 64) if m % t == 0)


def _mm(a, b, *, n_out, tm, tn, out_dtype, layer=None, col0=0, act=None, bias=None,
        res=None, scale=None, n_ctx_blocks=1):
    M, K = a.shape
    assert M % tm == 0 and n_out % tn == 0
    grid = (n_out // tn, M // tm)
    in_specs = [pl.BlockSpec((tm, K), lambda j, i: (i, 0))]
    if layer is None:
        in_specs.append(pl.BlockSpec((K, tn), lambda j, i: (0, j + col0)))
    else:
        in_specs.append(pl.BlockSpec((None, K, tn), lambda j, i: (layer, 0, j + col0)))
    args = [a, b]
    if bias is not None:
        in_specs.append(pl.BlockSpec((1, tn), lambda j, i: (0, j)))
        args.append(bias)
    if res is not None:
        in_specs.append(pl.BlockSpec((tm, tn), lambda j, i: (i, j)))
        in_specs.append(pl.BlockSpec((None, 1, tn), lambda j, i: (_row_class(i, n_ctx_blocks), 0, j)))
        args += [res, scale]
    return pl.pallas_call(
        functools.partial(_mm_body, act=act, has_bias=bias is not None, has_res=res is not None),
        grid=grid,
        in_specs=in_specs,
        out_specs=pl.BlockSpec((tm, tn), lambda j, i: (i, j)),
        out_shape=jax.ShapeDtypeStruct((M, n_out), out_dtype),
        scratch_shapes=[pltpu.VMEM((K, tn), BF16)],
        compiler_params=_cparams(2),
    )(*args)


def _norm_body(*refs, shift_idx, scale_idx, with_router):
    if with_router:
        x_ref, g_ref, mod_ref, rw_ref, o_ref, lg_ref = refs
    else:
        x_ref, g_ref, mod_ref, o_ref = refs
    x = x_ref[...]
    y = x * lax.rsqrt(jnp.mean(x * x, axis=-1, keepdims=True) + NORM_EPS) * g_ref[...]
    if scale_idx is not None:
        m = mod_ref[...]
        y = y * (1.0 + m[scale_idx:scale_idx + 1]) + m[shift_idx:shift_idx + 1]
    o_ref[...] = y.astype(o_ref.dtype)
    if with_router:
        lg_ref[...] = lax.dot_general(rw_ref[...], y, (((1,), (1,)), ((), ())), precision=HIGHEST,
                                      preferred_element_type=F32)


def _norm_mod(x, g, mod, *, shift_idx, scale_idx, out_dtype, n_ctx_blocks, router_w=None):
    S, D = x.shape
    tm = ROW_BLOCK
    in_specs = [pl.BlockSpec((tm, D), lambda i: (i, 0)),
                pl.BlockSpec((1, D), lambda i: (0, 0)),
                pl.BlockSpec((None, 8, D), lambda i: (_row_class(i, n_ctx_blocks), 0, 0))]
    args = [x, g, mod]
    out_specs = pl.BlockSpec((tm, D), lambda i: (i, 0))
    out_shape = jax.ShapeDtypeStruct((S, D), out_dtype)
    if router_w is not None:
        in_specs.append(pl.BlockSpec((N_EXPERTS, D), lambda i: (0, 0)))
        args.append(router_w.T)
        out_specs = [out_specs, pl.BlockSpec((N_EXPERTS, tm), lambda i: (0, i))]
        out_shape = [out_shape, jax.ShapeDtypeStruct((N_EXPERTS, S), F32)]
    return pl.pallas_call(
        functools.partial(_norm_body, shift_idx=shift_idx, scale_idx=scale_idx,
                          with_router=router_w is not None),
        grid=(S // tm,),
        in_specs=in_specs,
        out_specs=out_specs,
        out_shape=out_shape,
        compiler_params=_cparams(1),
    )(*args)


def _dft_tables(n, scale):
    k = jnp.arange(n, dtype=jnp.int32)
    ang = ((k[:, None] * k[None, :]) % n).astype(F32) * (2.0 * math.pi / n)
    return jnp.cos(ang) * scale, jnp.sin(ang) * scale


def _channel_dft_tables():
    c, s = _dft_tables(FN_GC, FN_GC ** -0.5)
    eye = jnp.eye(FN_GROUPS, dtype=F32)
    return jnp.kron(eye, c), jnp.kron(eye, s)


def _fnet_dense_body(u_ref, cc_ref, sc_ref, cl_ref, sl_ref, o_ref):
    u = u_ref[...]
    ac = _dot_hi(u, cc_ref[...])
    as_ = _dot_hi(u, sc_ref[...])
    o_ref[...] = _dot_hi(cl_ref[...], ac) - _dot_hi(sl_ref[...], as_)


def _fnet_dense(u, cc, sc, cl, sl):
    L = u.shape[0]
    return pl.pallas_call(
        _fnet_dense_body,
        out_shape=jax.ShapeDtypeStruct((L, BR), F32),
        compiler_params=pltpu.CompilerParams(vmem_limit_bytes=VMEM_LIMIT),
    )(u, cc, sc, cl, sl)


def _fnet_stage1_body(u_ref, gr_ref, gi_ref, cc_ref, sc_ref, yr_ref, yi_ref):
    u = u_ref[...]
    zr = _dot3(u, cc_ref[...])
    zi = -_dot3(u, sc_ref[...])
    gr, gi = gr_ref[...], gi_ref[...]
    yr_ref[...] = _dot3(gr, zr) - _dot3(gi, zi)
    yi_ref[...] = _dot3(gr, zi) + _dot3(gi, zr)


def _fnet_stage2_body(yr_ref, yi_ref, c2_ref, s2_ref, o_ref):
    c2, s2 = c2_ref[...], s2_ref[...]
    for j in range(8):
        o_ref[:, j, :] = _dot3(c2, yr_ref[j]) + _dot3(s2, yi_ref[j])


def _fnet_tables(n_ctx, n_lat):
    cc, sc = _channel_dft_tables()
    cl, sl = _dft_tables(n_ctx, n_ctx ** -0.5)
    n2 = FFT_N2
    n1 = n_lat // n2
    assert n1 * n2 == n_lat and n1 % 8 == 0
    f1 = jnp.arange(n1, dtype=jnp.int32)
    t1 = jnp.arange(n1, dtype=jnp.int32)
    t2 = jnp.arange(n2, dtype=jnp.int32)
    t = n2 * t1[None, None, :] + t2[:, None, None]
    ang = ((f1[None, :, None] * t) % n_lat).astype(F32) * (2.0 * math.pi / n_lat)
    gr = jnp.cos(ang) * (n_lat ** -0.5)
    gi = -jnp.sin(ang) * (n_lat ** -0.5)
    c2, s2 = _dft_tables(n2, 1.0)
    return dict(cc=cc, sc=sc, cl=cl, sl=sl, gr=gr, gi=gi, c2=c2, s2=s2)


def _fnet_two_stage(u, cc, sc, gr, gi, c2, s2):
    L = u.shape[0]
    n2 = FFT_N2
    n1 = L // n2
    u2 = u.reshape(n1, n2 * BR)
    yr, yi = pl.pallas_call(
        _fnet_stage1_body,
        grid=(n2,),
        in_specs=[pl.BlockSpec((n1, BR), lambda j: (0, j)),
                  pl.BlockSpec((None, n1, n1), lambda j: (j, 0, 0)),
                  pl.BlockSpec((None, n1, n1), lambda j: (j, 0, 0)),
                  pl.BlockSpec((BR, BR), lambda j: (0, 0)),
                  pl.BlockSpec((BR, BR), lambda j: (0, 0))],
        out_specs=[pl.BlockSpec((n1, BR), lambda j: (0, j)),
                   pl.BlockSpec((n1, BR), lambda j: (0, j))],
        out_shape=[jax.ShapeDtypeStruct((n1, n2 * BR), F32)] * 2,
        compiler_params=_cparams(1),
    )(u2, gr, gi, cc, sc)
    out = pl.pallas_call(
        _fnet_stage2_body,
        grid=(n1 // 8,),
        in_specs=[pl.BlockSpec((8, n2, BR), lambda i: (i, 0, 0)),
                  pl.BlockSpec((8, n2, BR), lambda i: (i, 0, 0)),
                  pl.BlockSpec((n2, n2), lambda i: (0, 0)),
                  pl.BlockSpec((n2, n2), lambda i: (0, 0))],
        out_specs=pl.BlockSpec((n2, 8, BR), lambda i: (0, i, 0)),
        out_shape=jax.ShapeDtypeStruct((n2, n1, BR), F32),
        compiler_params=_cparams(1),
    )(yr.reshape(n1, n2, BR), yi.reshape(n1, n2, BR), c2, s2)
    return out.reshape(L, BR)


def _fnet_mixer(cols, n_ctx, tb):
    ua = cols[:, COL_A * BR:(COL_A + 1) * BR]
    y_ctx = _fnet_dense(ua[:n_ctx], tb["cc"], tb["sc"], tb["cl"], tb["sl"])
    y_lat = _fnet_two_stage(ua[n_ctx:], tb["cc"], tb["sc"], tb["gr"], tb["gi"], tb["c2"], tb["s2"])
    return jnp.concatenate([y_ctx, y_lat], axis=0).astype(BF16)


def _halo_rows(x, c0, c1, n_ctx, tb, before, after):
    C = c1 - c0
    blocks = x.reshape(-1, tb, x.shape[1])
    ncb = n_ctx // tb
    parts = []
    for blk in (blocks[:ncb], blocks[ncb:]):
        prev = jnp.concatenate([jnp.zeros((1, before, C), x.dtype), blk[:-1, tb - before:, c0:c1]], axis=0)
        nxt = jnp.concatenate([blk[1:, :after, c0:c1], jnp.zeros((1, after, C), x.dtype)], axis=0)
        pad = jnp.zeros((blk.shape[0], 8 - before - after, C), x.dtype)
        parts.append(jnp.concatenate([prev, nxt, pad], axis=1))
    return jnp.concatenate(parts, axis=0)


def _shift_rows(x, halo_row, down):
    n = x.shape[0]
    row = lax.broadcasted_iota(jnp.int32, (n, 1), 0)
    if down:
        return jnp.where(row == 0, halo_row, pltpu.roll(x, 1, axis=0))
    return jnp.where(row == n - 1, halo_row, pltpu.roll(x, n - 1, axis=0))


def _lru_prep_body(x_ref, halo_ref, cw_ref, cb_ref, w4_ref, b4_ref, lam_ref,
                   af_ref, bf_ref, ab_ref, bb_ref):
    x = x_ref[...]
    halo = halo_ref[...]
    x_m1 = _shift_rows(x, halo[1:2], True)
    x_m2 = _shift_rows(x_m1, halo[0:1], True)
    x_p1 = _shift_rows(x, halo[2:3], False)
    cw = cw_ref[...]
    xc = cb_ref[...] + x_m2 * cw[0:1] + x_m1 * cw[1:2] + x * cw[2:3] + x_p1 * cw[3:4]
    pre = _sigmoid(_dot(xc, w4_ref[...]) + b4_ref[...])
    sp = _softplus(-lam_ref[...])
    for d, (a_ref, b_ref) in enumerate(((af_ref, bf_ref), (ab_ref, bb_ref))):
        r = pre[:, d * BR:(d + 1) * BR]
        i = pre[:, (2 + d) * BR:(3 + d) * BR]
        log_a = (-LRU_C) * r * sp[d:d + 1]
        a_ref[...] = jnp.exp(log_a)
        b_ref[...] = jnp.sqrt(jnp.maximum(1.0 - jnp.exp(2.0 * log_a), 0.0)) * i * xc


def _lru_scan_body(af_ref, bf_ref, ab_ref, bb_ref, hf_ref, hb_ref, sf_ref, sb_ref):
    @pl.when(pl.program_id(0) == 0)
    def _init():
        sf_ref[...] = jnp.zeros_like(sf_ref)
        sb_ref[...] = jnp.zeros_like(sb_ref)

    tb = af_ref.shape[0]

    def step(t, carry):
        hf, hb = carry
        hf = af_ref[pl.ds(t, 1), :] * hf + bf_ref[pl.ds(t, 1), :]
        hf_ref[pl.ds(t, 1), :] = hf
        u = tb - 1 - t
        hb = ab_ref[pl.ds(u, 1), :] * hb + bb_ref[pl.ds(u, 1), :]
        hb_ref[pl.ds(u, 1), :] = hb
        return hf, hb

    hf, hb = lax.fori_loop(0, tb, step, (sf_ref[...], sb_ref[...]), unroll=8)
    sf_ref[...] = hf
    sb_ref[...] = hb


def _gelu_tanh(x):
    return 0.5 * x * (1.0 + jnp.tanh(math.sqrt(2.0 / math.pi) * (x + 0.044715 * (x * x * x))))


def _lru_post_body(hf_ref, hb_ref, g_ref, o_ref):
    o_ref[...] = ((hf_ref[...] + hb_ref[...]) * _gelu_tanh(g_ref[...])).astype(o_ref.dtype)


def _block_diag(w):
    G, n, m = w.shape
    eye = jnp.eye(G, dtype=w.dtype)
    return (eye[:, None, :, None] * w[:, :, None, :]).reshape(G * n, G * m)


def _rglru_mixer(cols, n_ctx, conv_w, conv_b, wa, ba, wx, bx, lam):
    S = cols.shape[0]
    tb = ROW_BLOCK
    nblk, ncb = S // tb, n_ctx // tb
    halo = _halo_rows(cols, COL_D * BR, (COL_D + 1) * BR, n_ctx, tb, 2, 1)
    w4 = jnp.concatenate([_block_diag(wa[0]), _block_diag(wa[1]),
                          _block_diag(wx[0]), _block_diag(wx[1])], axis=1).astype(BF16)
    b4 = jnp.concatenate([ba[0], ba[1], bx[0], bx[1]])[None]
    cw8 = jnp.zeros((8, BR), F32).at[:4].set(conv_w)
    lam8 = jnp.zeros((8, BR), F32).at[:2].set(lam)
    row = lambda c: pl.BlockSpec((tb, BR), lambda i: (i, c))
    full = lambda shp: pl.BlockSpec(shp, lambda i: (0,) * len(shp))
    a_f, b_f, a_b, b_b = pl.pallas_call(
        _lru_prep_body,
        grid=(nblk,),
        in_specs=[row(COL_D), pl.BlockSpec((None, 8, BR), lambda i: (i, 0, 0)),
                  full((8, BR)), full((1, BR)), full((BR, 4 * BR)), full((1, 4 * BR)), full((8, BR))],
        out_specs=[row(0)] * 4,
        out_shape=[jax.ShapeDtypeStruct((S, BR), F32)] * 4,
        compiler_params=_cparams(1),
    )(cols, halo, cw8, conv_b[None], w4, b4, lam8)
    fwd = pl.BlockSpec((tb, BR), lambda g: (g, 0))
    bwd = pl.BlockSpec((tb, BR), lambda g: (_rev_block(g, ncb, nblk), 0))
    h_f, h_b = pl.pallas_call(
        _lru_scan_body,
        grid=(nblk,),
        in_specs=[fwd, fwd, bwd, bwd],
        out_specs=[fwd, bwd],
        out_shape=[jax.ShapeDtypeStruct((S, BR), F32)] * 2,
        scratch_shapes=[pltpu.VMEM((1, BR), F32), pltpu.VMEM((1, BR), F32)],
        compiler_params=_cparams(1),
    )(a_f, b_f, a_b, b_b)
    return pl.pallas_call(
        _lru_post_body,
        grid=(nblk,),
        in_specs=[row(0), row(0), row(COL_D + 1)],
        out_specs=row(0),
        out_shape=jax.ShapeDtypeStruct((S, BR), BF16),
        compiler_params=_cparams(1),
    )(h_f, h_b, cols)


def _level_sizes(c):
    out, b = [], c // 2
    while b >= 1:
        out.append(b)
        b //= 2
    return out


def _hgrn_tables(c):
    t = np.arange(c)
    sizes = _level_sizes(c)
    mats = np.zeros((2, (1 + len(sizes)) * c, c), np.float32)
    masks = np.zeros((len(sizes), c, c), np.float32)
    cum = [(t[None, :] <= t[:, None]).astype(np.float32), (t[None, :] >= t[:, None]).astype(np.float32)]
    for d in range(2):
        mats[d, :c] = cum[d]
        for l, b in enumerate(sizes):
            ref = (t // (2 * b)) * (2 * b) + (b - 1 if d == 0 else b)
            mats[d, (l + 1) * c:(l + 2) * c] = cum[d] - cum[d][ref]
    for l, b in enumerate(sizes):
        masks[l] = (t[:, None] // (2 * b) == t[None, :] // (2 * b)).astype(np.float32)
    return jnp.asarray(mats), jnp.asarray(masks)


def _hgrn_prepare(chunks, lb_ref, masks_ref):
    c = chunks[0][0].shape[0]
    sizes = _level_sizes(c)
    log_lb, log_1mlb, one_m_lb = lb_ref[0:1], lb_ref[1:2], lb_ref[2:3]
    row = lax.broadcasted_iota(jnp.int32, (c, 1), 0)
    eye = (lax.broadcasted_iota(jnp.int32, (c, c), 0) == lax.broadcasted_iota(jnp.int32, (c, c), 1))
    items = []
    for q_raw, z, v, mats, rev in chunks:
        q = _silu(q_raw)
        a = log_lb
        b = log_1mlb + _log_sigmoid(z)
        lf = jnp.maximum(a, b) + jnp.log1p(jnp.exp(-jnp.abs(a - b)))
        k = one_m_lb * _sigmoid(-z)
        dall = _dot_hi(mats, lf)
        g = dall[:c]
        g_last = g[0:1] if rev else g[c - 1:c]
        qg = (q * jnp.exp(g)).astype(BF16)
        kg = (k * jnp.exp(g_last - g)).astype(BF16)
        dec = jnp.exp(g_last)
        q_lv, k_lv = [], []
        for l, b in enumerate(sizes):
            dl = dall[(l + 1) * c:(l + 2) * c]
            second = ((row >> int(math.log2(b))) & 1) == 1
            q_side = jnp.logical_not(second) if rev else second
            k_side = jnp.logical_not(q_side)
            q_lv.append((q * jnp.exp(jnp.where(q_side, dl, NEG_BIG))).astype(BF16))
            k_lv.append((k * jnp.exp(jnp.where(k_side, -dl, NEG_BIG))).astype(BF16))
        qk = q * k
        vb = v.astype(BF16)
        for h in range(HG_HEADS):
            sl = slice(h * HG_DK, (h + 1) * HG_DK)
            items.append(dict(qg=qg[:, sl], kg=kg[:, sl], dec=dec[:, sl], v=vb[:, sl],
                              diag=jnp.sum(qk[:, sl], axis=-1, keepdims=True),
                              q_lv=[x[:, sl] for x in q_lv], k_lv=[x[:, sl] for x in k_lv]))
    prods = [[_dot_nt(it["q_lv"][l], it["k_lv"][l]) for it in items] for l in range(len(sizes))]
    scores = []
    for i, it in enumerate(items):
        s = jnp.where(eye, it["diag"], 0.0)
        for l in range(len(sizes)):
            s = s + masks_ref[l] * prods[l][i]
        scores.append(s)
    y_intra = [_dot(s, it["v"]) for s, it in zip(scores, items)]
    kv = [_dot_tn(it["v"], it["kg"]) for it in items]
    out = [dict(qg=it["qg"], dec=it["dec"], y_intra=y, kv=x) for it, y, x in zip(items, y_intra, kv)]
    return [out[i * HG_HEADS:(i + 1) * HG_HEADS] for i in range(len(chunks))]


def _hgrn_apply(problems):
    items = [(g, st_ref, h) for heads, st_ref in problems for h, g in enumerate(heads)]
    sts = [st_ref[h] for _, st_ref, h in items]
    ys = [_dot_nt(g["qg"], st) + g["y_intra"] for (g, _, _), st in zip(items, sts)]
    for (g, st_ref, h), st in zip(items, sts):
        st_ref[h] = st * g["dec"] + g["kv"]
    return [jnp.concatenate(ys[i * HG_HEADS:(i + 1) * HG_HEADS], axis=-1) for i in range(len(problems))]


HG_SUB = 4


def _hgrn_scan_body(qf_ref, zf_ref, vf_ref, qb_ref, zb_ref, vb_ref, lb_ref, mats_ref, masks_ref,
                    yf_ref, yb_ref, sf_ref, sb_ref):
    @pl.when(pl.program_id(0) == 0)
    def _init():
        sf_ref[...] = jnp.zeros_like(sf_ref)
        sb_ref[...] = jnp.zeros_like(sb_ref)

    c = CHUNK
    rows = lambda j: pl.ds(j * c, c)
    chunks = [(qf_ref[rows(j), :], zf_ref[rows(j), :], vf_ref[rows(j), :], mats_ref[0], False)
              for j in range(HG_SUB)]
    chunks += [(qb_ref[rows(j), :], zb_ref[rows(j), :], vb_ref[rows(j), :], mats_ref[1], True)
               for j in range(HG_SUB)]
    prep = _hgrn_prepare(chunks, lb_ref, masks_ref)
    for j in range(HG_SUB):
        jb = HG_SUB - 1 - j
        y_f, y_b = _hgrn_apply([(prep[j], sf_ref), (prep[HG_SUB + jb], sb_ref)])
        yf_ref[rows(j), :] = y_f
        yb_ref[rows(jb), :] = y_b


def _hgrn_post_body(yf_ref, yb_ref, og_ref, ng_ref, o_ref):
    y = yf_ref[...] + yb_ref[...]
    outs = []
    for h in range(HG_HEADS):
        yh = y[:, h * HG_DK:(h + 1) * HG_DK]
        outs.append(yh * lax.rsqrt(jnp.mean(yh * yh, axis=-1, keepdims=True) + NORM_EPS) * ng_ref[...])
    o_ref[...] = (jnp.concatenate(outs, axis=-1) * _silu(og_ref[...])).astype(o_ref.dtype)


def _hgrn2_mixer(cols, n_ctx, lb, norm_g):
    S = cols.shape[0]
    rows = HG_SUB * CHUNK
    nch, ncc = S // rows, n_ctx // rows
    mats, masks = _hgrn_tables(CHUNK)
    lb8 = jnp.zeros((8, BR), F32).at[0].set(jnp.log(lb)).at[1].set(jnp.log1p(-lb)).at[2].set(1.0 - lb)
    fwd = lambda col: pl.BlockSpec((rows, BR), lambda g: (g, col))
    bwd = lambda col: pl.BlockSpec((rows, BR), lambda g: (_rev_block(g, ncc, nch), col))
    full = lambda shp: pl.BlockSpec(shp, lambda g: (0,) * len(shp))
    y_f, y_b = pl.pallas_call(
        _hgrn_scan_body,
        grid=(nch,),
        in_specs=[fwd(COL_C), fwd(COL_C + 1), fwd(COL_C + 3),
                  bwd(COL_C), bwd(COL_C + 2), bwd(COL_C + 3),
                  full((8, BR)), full(mats.shape), full(masks.shape)],
        out_specs=[fwd(0), bwd(0)],
        out_shape=[jax.ShapeDtypeStruct((S, BR), F32)] * 2,
        scratch_shapes=[pltpu.VMEM((HG_HEADS, HG_DK, HG_DK), F32)] * 2,
        compiler_params=_cparams(1),
    )(cols, cols, cols, cols, cols, cols, lb8, mats, masks)
    tb = ROW_BLOCK
    row = lambda col: pl.BlockSpec((tb, BR), lambda i: (i, col))
    return pl.pallas_call(
        _hgrn_post_body,
        grid=(S // tb,),
        in_specs=[row(0), row(0), row(COL_C + 4), pl.BlockSpec((1, HG_DK), lambda i: (0, 0))],
        out_specs=row(0),
        out_shape=jax.ShapeDtypeStruct((S, BR), BF16),
        compiler_params=_cparams(1),
    )(y_f, y_b, cols, norm_g[None])


def _rw_prep_body(r_ref, k_ref, v_ref, x_ref, halo_ref, mu_ref, vec_ref, w1_ref, w2_ref, a1_ref,
                  a2_ref, g1_ref, g2_ref, bd_ref,
                  ro_ref, vo_ref, kko_ref, go_ref, wl0_ref, wl1_ref, kt0_ref, kt1_ref,
                  a0o_ref, a1o_ref):
    halo = halo_ref[...]
    mu = mu_ref[...]
    vec = vec_ref[...]

    def shifted(ref, j):
        u = ref[...]
        prev = _shift_rows(u, halo[0:1, j * BR:(j + 1) * BR], True)
        nxt = _shift_rows(u, halo[1:2, j * BR:(j + 1) * BR], False)
        return u, 0.5 * (prev + nxt) - u

    ur, xr = shifted(r_ref, 0)
    uk, xk = shifted(k_ref, 1)
    uv, xv = shifted(v_ref, 2)
    ux, xx = shifted(x_ref, 3)
    r = ur + xr * mu[0:1]
    k = uk + xk * mu[1:2]
    v = uv + xv * mu[2:3]
    xw = ux + xx * mu[3:4]
    xa = ux + xx * mu[4:5]
    xg = ux + xx * mu[5:6]
    kk = k * vec[0:1]
    ss = _dot3(kk * kk, bd_ref[...])
    kk = kk / jnp.maximum(jnp.sqrt(ss), 1e-12)
    ro_ref[...] = r
    vo_ref[...] = v
    kko_ref[...] = kk
    go_ref[...] = _dot(_sigmoid(_dot(xg, g1_ref[...])), g2_ref[...])
    w_raw = _dot(jnp.tanh(_dot(xw, w1_ref[...])), w2_ref[...])
    a_raw = _dot(_dot(xa, a1_ref[...]), a2_ref[...])
    k_a = vec[1:2]
    for d, (wl_ref, kt_ref, ao_ref) in enumerate(((wl0_ref, kt0_ref, a0o_ref),
                                                   (wl1_ref, kt1_ref, a1o_ref))):
        w_d = vec[2 + d:3 + d] + w_raw[:, d * BR:(d + 1) * BR]
        a_d = _sigmoid(vec[4 + d:5 + d] + a_raw[:, d * BR:(d + 1) * BR])
        wl_ref[...] = -jnp.exp(-_softplus(-w_d) - 0.5)
        kt_ref[...] = k * (1.0 + (a_d - 1.0) * k_a)
        ao_ref[...] = a_d


def _dot3(a, b):
    a_hi = a.astype(BF16)
    b_hi = b.astype(BF16)
    a_lo = (a - a_hi.astype(F32)).astype(BF16)
    b_lo = (b - b_hi.astype(F32)).astype(BF16)
    d = lambda p, q: jnp.dot(p, q, preferred_element_type=F32)
    return d(a_hi, b_hi) + (d(a_hi, b_lo) + d(a_lo, b_hi))


TRI_BASE = 2
RW_GROUP = 4
RW_GW = RW_GROUP * RW_HD


def _bdiag(x):
    n = x.shape[0]
    tiled = jnp.concatenate([x] * RW_GROUP, axis=0)
    ri = lax.broadcasted_iota(jnp.int32, tiled.shape, 0)
    ci = lax.broadcasted_iota(jnp.int32, tiled.shape, 1)
    blk = x.shape[1] // RW_GROUP
    same_block = (ri >> int(math.log2(n))) == (ci >> int(math.log2(blk)))
    return jnp.where(same_block, tiled, jnp.zeros_like(tiled))


def _mm_cat(a_cat, b_cat, dot=_dot):
    return dot(a_cat, _bdiag(b_cat))


def _tri_inverse_cat(l_cats, c):
    shape = l_cats[0].shape
    ri = lax.broadcasted_iota(jnp.int32, shape, 0)
    ci = lax.broadcasted_iota(jnp.int32, shape, 1) & (c - 1)
    same = lambda b: (ri >> int(math.log2(b))) == (ci >> int(math.log2(b)))
    eye = (ri == ci).astype(F32)
    ps = [jnp.where(same(TRI_BASE), l, 0.0) for l in l_cats]
    invs = [eye - p for p in ps]
    n = 2
    while n < TRI_BASE:
        ps = [_mm_cat(p, p, _dot3) for p in ps]
        invs = [inv + _mm_cat(inv, p, _dot3) for inv, p in zip(invs, ps)]
        n *= 2
    b = TRI_BASE
    while b < c:
        off_mask = jnp.logical_and(same(2 * b), jnp.logical_not(same(b)))
        t1 = [_mm_cat(inv, jnp.where(off_mask, l, 0.0)) for inv, l in zip(invs, l_cats)]
        t2 = [_mm_cat(t, inv) for t, inv in zip(t1, invs)]
        invs = [inv - t for inv, t in zip(invs, t2)]
        b *= 2
    return invs


def _rw_chunk_prepare(chunks):
    c = chunks[0][0].shape[0]
    assert c == RW_HD
    ri = lax.broadcasted_iota(jnp.int32, (c, RW_GW), 0)
    ci = lax.broadcasted_iota(jnp.int32, (c, RW_GW), 1) & (c - 1)
    n_grp = RW_HEADS // RW_GROUP
    items = []
    for r, wl, kt, v, kk, a, cum, rev in chunks:
        lc = _dot_hi(cum, wl)
        e_neg = jnp.exp(-lc)
        alpha = kk * jnp.exp(lc - wl)
        beta = a * kk * e_neg
        gamma = kt * e_neg
        rho = r * jnp.exp(lc)
        p_end = jnp.exp(lc[0:1] if rev else lc[c - 1:c])
        for grp in range(n_grp):
            sl = slice(grp * RW_GW, (grp + 1) * RW_GW)
            items.append(dict(al=alpha[:, sl], be=beta[:, sl], ga=gamma[:, sl], rh=rho[:, sl],
                              vv=v[:, sl], p_end=p_end[:, sl], rev=rev))
    gms = [_dot_nt(jnp.concatenate([it["al"], it["rh"]], axis=0),
                   jnp.concatenate([_bdiag(it["be"]), _bdiag(it["ga"])], axis=0)) for it in items]
    for it, gm in zip(items, gms):
        strict = (ci > ri) if it["rev"] else (ci < ri)
        incl = (ci >= ri) if it["rev"] else (ci <= ri)
        it["l_b"] = jnp.where(strict, gm[:c, :RW_GW], 0.0)
        it["l_g"] = jnp.where(strict, gm[:c, RW_GW:], 0.0)
        it["r_b"] = jnp.where(incl, gm[c:, :RW_GW], 0.0).astype(BF16)
        it["r_g"] = jnp.where(incl, gm[c:, RW_GW:], 0.0)
    lgv = [_mm_cat(it["l_g"], it["vv"]) for it in items]
    y0 = [_mm_cat(it["r_g"], it["vv"]) for it in items]
    upd0 = [_dot_tn(it["vv"], it["ga"]) for it in items]
    t_invs = _tri_inverse_cat([it["l_b"] for it in items], c)
    ws = [_mm_cat(t, it["al"]) for t, it in zip(t_invs, items)]
    u0 = [_mm_cat(t, x) for t, x in zip(t_invs, lgv)]
    out = [dict(w_rho=jnp.concatenate([w, it["rh"]], axis=0).astype(BF16), u0=u, y0=y,
                r_b=it["r_b"], be=it["be"].astype(BF16), upd0=up, p_end=it["p_end"])
           for it, w, u, y, up in zip(items, ws, u0, y0, upd0)]
    return [out[i * n_grp:(i + 1) * n_grp] for i in range(len(chunks))]


def _rw_chunk_apply(problems):
    bi = lax.broadcasted_iota(jnp.int32, (RW_GW, RW_GW), 0) >> int(math.log2(RW_HD))
    bj = lax.broadcasted_iota(jnp.int32, (RW_GW, RW_GW), 1) >> int(math.log2(RW_HD))
    diag_blocks = bi == bj
    items = [(g, st_ref, grp) for groups, st_ref in problems for grp, g in enumerate(groups)]
    sts = [st_ref[grp] for _, st_ref, grp in items]
    wss = [_dot_nt(g["w_rho"], st) for (g, _, _), st in zip(items, sts)]
    c = items[0][0]["u0"].shape[0]
    us = [ws[:c] + g["u0"] for (g, _, _), ws in zip(items, wss)]
    rbu = [_mm_cat(g["r_b"], u) for (g, _, _), u in zip(items, us)]
    utb = [_dot_tn(u, g["be"]) for (g, _, _), u in zip(items, us)]
    ys = [ws[c:] + g["y0"] - x for (g, _, _), ws, x in zip(items, wss, rbu)]
    for (g, st_ref, grp), st, x in zip(items, sts, utb):
        st_ref[grp] = (st + jnp.where(diag_blocks, g["upd0"] - x, 0.0)) * g["p_end"]
    n_grp = RW_HEADS // RW_GROUP
    return [jnp.concatenate(ys[i * n_grp:(i + 1) * n_grp], axis=-1) for i in range(len(problems))]


RW_SUB = 4


def _rw_scan_body(rf_ref, wlf_ref, ktf_ref, vf_ref, kkf_ref, af_ref,
                  rb_ref, wlb_ref, ktb_ref, vb_ref, kkb_ref, ab_ref, cum_ref,
                  yf_ref, yb_ref, sf_ref, sb_ref):
    @pl.when(pl.program_id(0) == 0)
    def _init():
        sf_ref[...] = jnp.zeros_like(sf_ref)
        sb_ref[...] = jnp.zeros_like(sb_ref)

    c = CHUNK
    fwd_refs = (rf_ref, wlf_ref, ktf_ref, vf_ref, kkf_ref, af_ref)
    bwd_refs = (rb_ref, wlb_ref, ktb_ref, vb_ref, kkb_ref, ab_ref)
    rows = lambda j: pl.ds(j * c, c)
    chunks = [tuple(ref[rows(j), :] for ref in fwd_refs) + (cum_ref[0], False) for j in range(RW_SUB)]
    chunks += [tuple(ref[rows(j), :] for ref in bwd_refs) + (cum_ref[1], True) for j in range(RW_SUB)]
    prep = _rw_chunk_prepare(chunks)
    for j in range(RW_SUB):
        jb = RW_SUB - 1 - j
        y_f, y_b = _rw_chunk_apply([(prep[j], sf_ref), (prep[RW_SUB + jb], sb_ref)])
        yf_ref[rows(j), :] = y_f
        yb_ref[rows(jb), :] = y_b


def _rw_post_body(yf_ref, yb_ref, r_ref, kt0_ref, kt1_ref, v_ref, g_ref, vec_ref, bd_ref, o_ref):
    y = yf_ref[...] + yb_ref[...]
    bd = bd_ref[...]
    vec = vec_ref[...]
    mean = _dot3(y, bd) * (1.0 / RW_HD)
    yc = y - mean
    var = _dot3(yc * yc, bd) * (1.0 / RW_HD)
    yn = yc * lax.rsqrt(var + RW_GN_EPS) * vec[0:1] + vec[1:2]
    bonus = _dot3(r_ref[...] * (kt0_ref[...] + kt1_ref[...]) * vec[2:3], bd) * v_ref[...]
    o_ref[...] = ((yn + bonus) * g_ref[...]).astype(o_ref.dtype)


def _rwkv7_mixer(cols, n_ctx, mu, w0, w1, w2, a0, a1, a2, g1, g2, k_k, k_a, r_k, ln_g, ln_b):
    S = cols.shape[0]
    tb = ROW_BLOCK
    nblk = S // tb
    halo = _halo_rows(cols, COL_B * BR, (COL_B + 4) * BR, n_ctx, tb, 1, 1)
    mu8 = jnp.zeros((8, BR), F32).at[:6].set(mu)
    vec = jnp.zeros((8, BR), F32).at[0].set(k_k).at[1].set(k_a).at[2:4].set(w0).at[4:6].set(a0)
    w1c = jnp.concatenate([w1[0], w1[1]], axis=1).astype(BF16)
    a1c = jnp.concatenate([a1[0], a1[1]], axis=1).astype(BF16)
    w2c = _block_diag(w2).astype(BF16)
    a2c = _block_diag(a2).astype(BF16)
    bd = jnp.kron(jnp.eye(RW_HEADS, dtype=F32), jnp.ones((RW_HD, RW_HD), F32))
    row = lambda col: pl.BlockSpec((tb, BR), lambda i: (i, col))
    full = lambda arr: pl.BlockSpec(arr.shape, lambda i: (0,) * arr.ndim)
    consts = [mu8, vec, w1c, w2c, a1c, a2c, g1.astype(BF16), g2.astype(BF16), bd]
    outs = pl.pallas_call(
        _rw_prep_body,
        grid=(nblk,),
        in_specs=[row(COL_B), row(COL_B + 1), row(COL_B + 2), row(COL_B + 3),
                  pl.BlockSpec((None, 8, 4 * BR), lambda i: (i, 0, 0))] + [full(x) for x in consts],
        out_specs=[row(0)] * 10,
        out_shape=[jax.ShapeDtypeStruct((S, BR), F32)] * 10,
        compiler_params=_cparams(1),
    )(cols, cols, cols, cols, halo, *consts)
    r, v, kk, g, wl0, wl1, kt0, kt1, av0, av1 = outs

    c = CHUNK
    rows = RW_SUB * c
    nch, ncc = S // rows, n_ctx // rows
    t = np.arange(c)
    cum = jnp.asarray(np.stack([(t[None, :] <= t[:, None]), (t[None, :] >= t[:, None])]).astype(np.float32))
    fwd = pl.BlockSpec((rows, BR), lambda i: (i, 0))
    bwd = pl.BlockSpec((rows, BR), lambda i: (_rev_block(i, ncc, nch), 0))
    y_f, y_b = pl.pallas_call(
        _rw_scan_body,
        grid=(nch,),
        in_specs=[fwd] * 6 + [bwd] * 6 + [pl.BlockSpec((2, c, c), lambda i: (0, 0, 0))],
        out_specs=[fwd, bwd],
        out_shape=[jax.ShapeDtypeStruct((S, BR), F32)] * 2,
        scratch_shapes=[pltpu.VMEM((RW_HEADS // RW_GROUP, RW_GW, RW_GW), F32)] * 2,
        compiler_params=_cparams(1),
    )(r, wl0, kt0, v, kk, av0, r, wl1, kt1, v, kk, av1, cum)

    pvec = jnp.zeros((8, BR), F32).at[0].set(ln_g).at[1].set(ln_b).at[2].set(r_k.reshape(-1))
    return pl.pallas_call(
        _rw_post_body,
        grid=(nblk,),
        in_specs=[row(0)] * 7 + [full(pvec), full(bd)],
        out_specs=row(0),
        out_shape=jax.ShapeDtypeStruct((S, BR), BF16),
        compiler_params=_cparams(1),
    )(y_f, y_b, r, kt0, kt1, v, g, pvec, bd)


def _merge_body(ya_ref, yb_ref, yc_ref, yd_ref, gate_ref, wb_ref, o_ref):
    d = o_ref.shape[1]
    acc = None
    for k, y_ref in enumerate((ya_ref, yb_ref, yc_ref, yd_ref)):
        p = jnp.dot(y_ref[...], wb_ref[k], preferred_element_type=F32)
        p = gate_ref[:, k * d:(k + 1) * d].astype(F32) * p
        acc = p if acc is None else acc + p
    o_ref[...] = acc.astype(o_ref.dtype)


def _merge(branches, gates, wb):
    S = gates.shape[0]
    d = wb.shape[2]
    tm = ROW_BLOCK
    row = pl.BlockSpec((tm, BR), lambda i: (i, 0))
    return pl.pallas_call(
        _merge_body,
        grid=(S // tm,),
        in_specs=[row] * 4 + [pl.BlockSpec((tm, N_BRANCH * d), lambda i: (i, 0)),
                              pl.BlockSpec(wb.shape, lambda i: (0, 0, 0))],
        out_specs=pl.BlockSpec((tm, d), lambda i: (i, 0)),
        out_shape=jax.ShapeDtypeStruct((S, d), BF16),
        compiler_params=_cparams(1),
    )(*branches, gates, wb)


def _expert_body(blk_e_ref, n_used_ref, slot_ref, nxt_ref, src_ref, h_hbm, wg_hbm, wu_hbm, wd_hbm, o_ref,
                 xbuf_ref, wgf_ref, wuf_ref, wdf_ref, wgb_ref, wub_ref, wdb_ref, sem_ref, xsem_ref, *,
                 layer):
    b = pl.program_id(0)
    e = blk_e_ref[b]
    new_expert = jnp.logical_or(b == 0, e != blk_e_ref[jnp.maximum(b - 1, 0)])
    n_used = n_used_ref[0]
    used = b < n_used
    slot = slot_ref[b]
    xslot = b % 2

    def row_copy(blk, i, xs):
        r = src_ref[blk * MOE_BLOCK + i]
        return pltpu.make_async_copy(h_hbm.at[pl.ds(r, 1)], xbuf_ref.at[xs, pl.ds(i, 1)], xsem_ref.at[xs])

    def start_rows(blk, xs):
        for i in range(MOE_BLOCK):
            row_copy(blk, i, xs).start()

    @pl.when(b == 0)
    def _first_rows():
        start_rows(0, 0)

    @pl.when(b + 1 < n_used)
    def _next_rows():
        start_rows(b + 1, 1 - xslot)

    def weight_copies(expert, s):
        return (pltpu.make_async_copy(wg_hbm.at[layer, expert], wgf_ref.at[s], sem_ref.at[s, 0]),
                pltpu.make_async_copy(wu_hbm.at[layer, expert], wuf_ref.at[s], sem_ref.at[s, 1]),
                pltpu.make_async_copy(wd_hbm.at[layer, expert], wdf_ref.at[s], sem_ref.at[s, 2]))

    @pl.when(jnp.logical_and(used, new_expert))
    def _switch_expert():
        @pl.when(b == 0)
        def _first_fetch():
            for cp in weight_copies(e, slot):
                cp.start()

        nxt = nxt_ref[b]

        @pl.when(nxt >= 0)
        def _prefetch_next():
            for cp in weight_copies(nxt, 1 - slot):
                cp.start()

        for cp in weight_copies(e, slot):
            cp.wait()
        wgb_ref[...] = wgf_ref[slot].astype(BF16)
        wub_ref[...] = wuf_ref[slot].astype(BF16)
        wdb_ref[...] = wdf_ref[slot].astype(BF16)

    @pl.when(used)
    def _compute():
        pltpu.make_async_copy(h_hbm.at[pl.ds(0, MOE_BLOCK)], xbuf_ref.at[xslot], xsem_ref.at[xslot]).wait()
        x = xbuf_ref[xslot].astype(BF16)
        hg = jnp.dot(x, wgb_ref[...], preferred_element_type=F32)
        hu = jnp.dot(x, wub_ref[...], preferred_element_type=F32)
        hidden = (_silu(hg) * hu).astype(BF16)
        o_ref[...] = jnp.dot(hidden, wdb_ref[...], preferred_element_type=F32).astype(o_ref.dtype)

    @pl.when(jnp.logical_not(used))
    def _unused():
        o_ref[...] = jnp.zeros_like(o_ref)


def _expert_ffn(h, src, blk_e, n_used, w_gate, w_up, w_down, layer):
    D = h.shape[1]
    n_rows = src.shape[0]
    n_blocks = n_rows // MOE_BLOCK
    de = w_gate.shape[-1]
    blk = jnp.arange(n_blocks, dtype=jnp.int32)
    change = jnp.concatenate([jnp.zeros((1,), jnp.int32), (blk_e[1:] != blk_e[:-1]).astype(jnp.int32)])
    slot = (jnp.cumsum(change) % 2).astype(jnp.int32)
    nxt_blk = jnp.sum(blk_e[None, :] <= blk_e[:, None], axis=1).astype(jnp.int32)
    nxt = jnp.where(nxt_blk < n_used[0], blk_e[jnp.minimum(nxt_blk, n_blocks - 1)], -1).astype(jnp.int32)
    grid_spec = pltpu.PrefetchScalarGridSpec(
        num_scalar_prefetch=5,
        grid=(n_blocks,),
        in_specs=[pl.BlockSpec(memory_space=pl.ANY),
                  pl.BlockSpec(memory_space=pl.ANY),
                  pl.BlockSpec(memory_space=pl.ANY),
                  pl.BlockSpec(memory_space=pl.ANY)],
        out_specs=pl.BlockSpec((MOE_BLOCK, D), lambda b, *_: (b, 0)),
        scratch_shapes=[pltpu.VMEM((2, MOE_BLOCK, D), F32),
                        pltpu.VMEM((2, D, de), F32), pltpu.VMEM((2, D, de), F32), pltpu.VMEM((2, de, D), F32),
                        pltpu.VMEM((D, de), BF16), pltpu.VMEM((D, de), BF16), pltpu.VMEM((de, D), BF16),
                        pltpu.SemaphoreType.DMA((2, 3)), pltpu.SemaphoreType.DMA((2,))],
    )
    return pl.pallas_call(
        functools.partial(_expert_body, layer=layer),
        grid_spec=grid_spec,
        out_shape=jax.ShapeDtypeStruct((n_rows, D), BF16),
        compiler_params=_cparams(1),
    )(blk_e, n_used, slot, nxt, src, h, w_gate, w_up, w_down)


def _combine_body(x_ref, y0_ref, y1_ref, w_ref, mod_ref, o_ref, *, gate_idx):
    w = w_ref[...]
    y = y0_ref[...].astype(F32) * w[:, 0:1] + y1_ref[...].astype(F32) * w[:, 1:2]
    o_ref[...] = x_ref[...] + mod_ref[gate_idx:gate_idx + 1] * y


def _route_body(lg_ref, bias_ref, tri_ref, ints_ref, wts_ref, cnt_out_ref, cnt_ref):
    @pl.when(pl.program_id(0) == 0)
    def _init():
        cnt_ref[...] = jnp.zeros_like(cnt_ref)

    scores = _sigmoid(lg_ref[...])
    sel = scores + bias_ref[...]
    tb = scores.shape[1]
    epg = EXPERTS_PER_GROUP
    row = lax.broadcasted_iota(jnp.int32, (epg, tb), 0)
    best = e0 = e1 = w0 = w1 = None
    for g in range(N_GROUPS):
        s = sel[g * epg:(g + 1) * epg]
        sc = scores[g * epg:(g + 1) * epg]
        m1 = jnp.max(s, axis=0, keepdims=True)
        i1 = jnp.min(jnp.where(s == m1, row, epg), axis=0, keepdims=True)
        first = row == i1
        s2 = jnp.where(first, -jnp.inf, s)
        m2 = jnp.max(s2, axis=0, keepdims=True)
        i2 = jnp.min(jnp.where(s2 == m2, row, epg), axis=0, keepdims=True)
        second = row == i2
        gs = m1 + m2
        a0 = jnp.sum(jnp.where(first, sc, 0.0), axis=0, keepdims=True)
        a1 = jnp.sum(jnp.where(second, sc, 0.0), axis=0, keepdims=True)
        if g == 0:
            best, e0, e1, w0, w1 = gs, i1, i2, a0, a1
        else:
            take = gs > best
            best = jnp.where(take, gs, best)
            e0 = jnp.where(take, i1 + g * epg, e0)
            e1 = jnp.where(take, i2 + g * epg, e1)
            w0 = jnp.where(take, a0, w0)
            w1 = jnp.where(take, a1, w1)
    rows_e = lax.broadcasted_iota(jnp.int32, scores.shape, 0)
    oh0 = rows_e == e0
    oh1 = rows_e == e1
    oh = jnp.logical_or(oh0, oh1).astype(F32)
    before = jnp.dot(oh.astype(BF16), tri_ref[...], preferred_element_type=F32) + cnt_ref[:, 0:1]
    rank0 = jnp.sum(jnp.where(oh0, before, 0.0), axis=0, keepdims=True)
    rank1 = jnp.sum(jnp.where(oh1, before, 0.0), axis=0, keepdims=True)
    cnt_ref[...] = cnt_ref[...] + jnp.sum(oh, axis=1, keepdims=True)
    cnt_out_ref[...] = cnt_ref[...]
    wsum = w0 + w1
    zi = jnp.zeros((4, tb), jnp.int32)
    ints_ref[...] = jnp.concatenate([e0, e1, rank0.astype(jnp.int32), rank1.astype(jnp.int32), zi], axis=0)
    wts_ref[...] = jnp.concatenate([w0 / wsum, w1 / wsum, jnp.zeros((6, tb), F32)], axis=0)


def _route(logits_t, router_b):
    E, n = logits_t.shape
    tb = ROW_BLOCK
    t = np.arange(tb)
    tri = jnp.asarray((t[:, None] < t[None, :]).astype(np.float32), BF16)
    ints, wts, cnt = pl.pallas_call(
        _route_body,
        grid=(n // tb,),
        in_specs=[pl.BlockSpec((E, tb), lambda i: (0, i)),
                  pl.BlockSpec((E, 1), lambda i: (0, 0)),
                  pl.BlockSpec((tb, tb), lambda i: (0, 0))],
        out_specs=[pl.BlockSpec((8, tb), lambda i: (0, i)),
                   pl.BlockSpec((8, tb), lambda i: (0, i)),
                   pl.BlockSpec((E, 128), lambda i: (0, 0))],
        out_shape=[jax.ShapeDtypeStruct((8, n), jnp.int32), jax.ShapeDtypeStruct((8, n), F32),
                   jax.ShapeDtypeStruct((E, 128), F32)],
        scratch_shapes=[pltpu.VMEM((E, 128), F32)],
        compiler_params=_cparams(1),
    )(logits_t, router_b.astype(F32)[:, None], tri)
    return ints, wts, cnt[:, 0].astype(jnp.int32)


def _moe(xres, h, logits_t, router_b, w_gate, w_up, w_down, layer, mod, n_ctx_blocks, row0):
    S, D = xres.shape
    n = S - row0
    ints, wts, sizes = _route(logits_t[:, row0:], router_b)
    n_assign = n * TOP_K
    padded = ((sizes + MOE_BLOCK - 1) // MOE_BLOCK) * MOE_BLOCK
    pad_ends = jnp.cumsum(padded)
    pad_starts = pad_ends - padded
    dest0 = pad_starts[ints[0]] + ints[2]
    dest1 = pad_starts[ints[1]] + ints[3]
    n_blocks = -(-n_assign // MOE_BLOCK) + N_EXPERTS
    tok = jnp.arange(n, dtype=jnp.int32) + row0
    dest = jnp.concatenate([dest0, dest1])
    src = jnp.full((n_blocks * MOE_BLOCK,), row0, jnp.int32).at[dest].set(jnp.concatenate([tok, tok]))
    blk_start = jnp.arange(n_blocks, dtype=jnp.int32) * MOE_BLOCK
    blk_e = jnp.minimum(jnp.sum(pad_ends[None, :] <= blk_start[:, None], axis=1),
                        N_EXPERTS - 1).astype(jnp.int32)
    n_used = (pad_ends[-1] // MOE_BLOCK).astype(jnp.int32).reshape(1)
    yb = _expert_ffn(h, src, blk_e, n_used, w_gate, w_up, w_down, layer)
    yg = yb[dest]
    tm = ROW_BLOCK
    r0 = row0 // tm
    row = pl.BlockSpec((tm, D), lambda i: (i, 0))
    row2 = pl.BlockSpec((tm, D), lambda i: (i + n // tm, 0))
    xrow = pl.BlockSpec((tm, D), lambda i: (i + r0, 0))
    out = pl.pallas_call(
        functools.partial(_combine_body, gate_idx=5),
        grid=(n // tm,),
        in_specs=[xrow, row, row2, pl.BlockSpec((tm, 8), lambda i: (i, 0)),
                  pl.BlockSpec((None, 8, D), lambda i: (_row_class(i + r0, n_ctx_blocks), 0, 0))],
        out_specs=row,
        out_shape=jax.ShapeDtypeStruct((n, D), F32),
        compiler_params=_cparams(1),
    )(xres, yg, yg, wts.T, mod)
    if row0:
        out = jnp.concatenate([xres[:row0], out], axis=0)
    return out


def _grid_sincos(rows, d_model):
    nf = d_model // 4
    omega = POS_BASE ** (-jnp.arange(nf, dtype=F32) / nf)
    ang_r = jnp.arange(rows, dtype=F32)[:, None] * omega
    ang_c = jnp.arange(GRID_W, dtype=F32)[:, None] * omega
    f_r = jnp.concatenate([jnp.sin(ang_r), jnp.cos(ang_r)], axis=-1)[:, None, :]
    f_c = jnp.concatenate([jnp.sin(ang_c), jnp.cos(ang_c)], axis=-1)[None, :, :]
    shape = (rows, GRID_W, 2 * nf)
    pos = jnp.concatenate([jnp.broadcast_to(f_r, shape), jnp.broadcast_to(f_c, shape)], axis=-1)
    return pos.reshape(rows * GRID_W, d_model)


def kernel(x, c, ctx, c_ctx, w_mod, b_mod, norm_mix_g, norm_ffn_g, w_in, w_branch, w_out, rw_mu, rw_w0, rw_w1, rw_w2, rw_a0, rw_a1, rw_a2, rw_g1, rw_g2, rw_kk, rw_ka, rw_rk, rw_ln_g, rw_ln_b, hg_lb_logits, hg_norm_g, lru_conv_w, lru_conv_b, lru_wa, lru_ba, lru_wx, lru_bx, lru_lam, router_w, router_b, moe_w_gate, moe_w_up, moe_w_down, final_norm_g):
    _, T, D = x.shape
    n_ctx = ctx.shape[1]
    depth = w_in.shape[0]
    S = n_ctx + T
    ncb = n_ctx // ROW_BLOCK
    n_mix = N_COL_BLOCKS * BR

    xs = jnp.concatenate([ctx[0], x[0] + _grid_sincos(T // GRID_W, D)], axis=0)
    lb_cum = jnp.cumsum(jax.nn.softmax(hg_lb_logits.astype(F32), axis=0), axis=0)
    hg_lb = lb_cum - lb_cum[:1]
    cvec = jnp.zeros((8, D), F32).at[0].set(jax.nn.silu(c_ctx)).at[1].set(jax.nn.silu(c[0]))
    fnet_tables = _fnet_tables(n_ctx, T)

    for l in range(depth):
        last = l == depth - 1
        mod = _mm(cvec, w_mod, layer=l, n_out=N_MOD * D, tm=8, tn=1024, out_dtype=F32,
                  bias=b_mod[l][None])
        mod = jnp.zeros((2, 8, D), F32).at[:, :N_MOD].set(mod[:2].reshape(2, N_MOD, D))
        h = _norm_mod(xs, norm_mix_g[l][None], mod, shift_idx=0, scale_idx=1, out_dtype=BF16,
                      n_ctx_blocks=ncb)
        tm = _pick_tm(S)
        cols = _mm(h, w_in, layer=l, n_out=n_mix, tm=tm, tn=1024, out_dtype=F32)
        gates = _mm(h, w_in, layer=l, col0=n_mix // 1024, n_out=N_BRANCH * D, tm=tm, tn=1024,
                    out_dtype=BF16, act="sigmoid")
        y_a = _fnet_mixer(cols, n_ctx, fnet_tables)
        y_b = _rwkv7_mixer(cols, n_ctx, rw_mu[l], rw_w0[l], rw_w1[l], rw_w2[l], rw_a0[l], rw_a1[l],
                           rw_a2[l], rw_g1[l], rw_g2[l], rw_kk[l], rw_ka[l], rw_rk[l], rw_ln_g[l],
                           rw_ln_b[l])
        y_c = _hgrn2_mixer(cols, n_ctx, hg_lb[l], hg_norm_g[l])
        y_d = _rglru_mixer(cols, n_ctx, lru_conv_w[l], lru_conv_b[l], lru_wa[l], lru_ba[l],
                           lru_wx[l], lru_bx[l], lru_lam[l])
        merged = _merge((y_a, y_b, y_c, y_d), gates, w_branch[l].astype(BF16))
        xs = _mm(merged, w_out, layer=l, n_out=D, tm=ROW_BLOCK, tn=1024, out_dtype=F32,
                 res=xs, scale=mod[:, 2:3], n_ctx_blocks=ncb)
        h2, logits = _norm_mod(xs, norm_ffn_g[l][None], mod, shift_idx=3, scale_idx=4,
                               out_dtype=F32, n_ctx_blocks=ncb, router_w=router_w)
        xs = _moe(xs, h2, logits, router_b, moe_w_gate, moe_w_up, moe_w_down, l, mod, ncb,
                  n_ctx if last else 0)
    out = _norm_mod(xs[n_ctx:], final_norm_g[None], jnp.zeros((2, 8, D), F32), shift_idx=None,
                    scale_idx=None, out_dtype=F32, n_ctx_blocks=0)
    return out[None]
```
